```python
import math
import jax, jax.numpy as jnp
from jax import lax
import numpy as np


D_MODEL = 2048
BATCH = 1
SEQ = 16384
DEPTH = 2

N_MEM = 256
CONV_WIDTH = D_MODEL // 2
CONV_K = 3
HG_HEADS = 8
HG_KEY_DIM = 128
HG_VAL_DIM = D_MODEL // 2 // HG_HEADS
HG_FWIDTH = HG_HEADS * HG_KEY_DIM
HG_VWIDTH = HG_HEADS * HG_VAL_DIM
HG_CHUNK = 64
LOG_TINY = 1e-30
MEM_HEADS = 4
MEM_HEAD_DIM = D_MODEL // 2 // MEM_HEADS
MEM_WIDTH = MEM_HEADS * MEM_HEAD_DIM
BRANCH_WIDTH = D_MODEL // 2
N_BRANCH = 3
FFN_DIM = 256 * (-(-8 * D_MODEL // (3 * 256)))
RMS_EPS = 1e-6
IN_SPLITS = [CONV_WIDTH, CONV_WIDTH, CONV_WIDTH,
             HG_FWIDTH, HG_FWIDTH, HG_VWIDTH, HG_VWIDTH,
             MEM_WIDTH,
             D_MODEL, D_MODEL, D_MODEL]
N_IN = sum(IN_SPLITS)

kernel_name = 'hybrid_conv_hgrn2_memxattn_block'


def rms_norm(x, w):
    xf = x.astype(jnp.float32)
    y = xf * lax.rsqrt(jnp.mean(xf * xf, axis=-1, keepdims=True) + RMS_EPS)
    return (y * w.astype(jnp.float32)).astype(x.dtype)


def causal_dwconv3(u, w):
    s = u.shape[1]
    up = jnp.pad(u, ((0, 0), (CONV_K - 1, 0), (0, 0)))
    return up[:, :s] * w[0] + up[:, 1:s + 1] * w[1] + up[:, 2:] * w[2]


def hgrn2_chunked(q, k, v, log_f):
    bsz, s, h, dk = q.shape
    dv = v.shape[-1]
    n = s // HG_CHUNK

    def chunks(a):
        a = a.astype(jnp.float32).reshape(bsz, n, HG_CHUNK, h, a.shape[-1])
        return jnp.moveaxis(a, 1, 0)

    qc, kc, vc = chunks(q), chunks(k), chunks(v)
    bc = jnp.cumsum(chunks(log_f), axis=2)
    mask = jnp.tril(jnp.ones((HG_CHUNK, HG_CHUNK), dtype=bool))[None, :, :, None, None]

    def step(state, inp):
        qi, ki, vi, bi = inp
        o_inter = jnp.einsum('bchk,bhkv->bchv', qi * jnp.exp(bi), state)
        diff = bi[:, :, None] - bi[:, None, :]
        decay = jnp.where(mask, jnp.exp(jnp.minimum(diff, 0.0)), 0.0)
        scores = jnp.einsum('bihk,bijhk,bjhk->bhij', qi, decay, ki)
        o_intra = jnp.einsum('bhij,bjhv->bihv', scores, vi)
        b_last = bi[:, -1]
        k_to_end = ki * jnp.exp(b_last[:, None] - bi)
        state = jnp.exp(b_last)[..., None] * state + jnp.einsum('bjhk,bjhv->bhkv', k_to_end, vi)
        return state, o_inter + o_intra

    s0 = jnp.zeros((bsz, h, dk, dv), jnp.float32)
    _, o = lax.scan(step, s0, (qc, kc, vc, bc))
    return jnp.moveaxis(o, 0, 1).reshape(bsz, s, h, dv)


def setup_inputs(seed: int = 0) -> dict:
    key = jax.random.key(seed)
    ks = jax.random.split(key, 20)

    def nrm(k, shape, scale):
        return jax.random.normal(k, shape, jnp.float32) * scale

    def gain(k):
        return 1.0 + nrm(k, (DEPTH, D_MODEL), 0.02)

    return {
        'x': nrm(ks[0], (BATCH, SEQ, D_MODEL), 1.0),
        'mem': nrm(ks[1], (BATCH, N_MEM, D_MODEL), 1.0),
        'w_in': nrm(ks[2], (DEPTH, D_MODEL, N_IN), D_MODEL ** -0.5),
        'conv_mix_w': nrm(ks[3], (DEPTH, CONV_K, CONV_WIDTH), CONV_K ** -0.5),
        'hg_lower_bounds': nrm(ks[4], (DEPTH, HG_FWIDTH), 0.1),
        'hg_norm_w': 1.0 + nrm(ks[5], (DEPTH, HG_VWIDTH), 0.02),
        'w_mem_kv': nrm(ks[6], (DEPTH, D_MODEL, 2 * MEM_WIDTH), D_MODEL ** -0.5),
        'w_branch': nrm(ks[7], (DEPTH, N_BRANCH, BRANCH_WIDTH, D_MODEL), BRANCH_WIDTH ** -0.5),
        'w_out': nrm(ks[8], (DEPTH, D_MODEL, D_MODEL), D_MODEL ** -0.5),
        'norm_mix_pre': gain(ks[9]),
        'norm_mix_post': gain(ks[10]),
        'norm_mem': gain(ks[11]),
        'norm_ffn_pre': gain(ks[12]),
        'norm_ffn_post': gain(ks[13]),
        'w_ffn_up': nrm(ks[14], (DEPTH, D_MODEL, 2 * FFN_DIM), D_MODEL ** -0.5),
        'conv_ffn_w': nrm(ks[15], (DEPTH, CONV_K, FFN_DIM), CONV_K ** -0.5),
        'conv_ffn_b': nrm(ks[16], (DEPTH, FFN_DIM), 0.01),
        'w_ffn_down': nrm(ks[17], (DEPTH, FFN_DIM, D_MODEL), FFN_DIM ** -0.5),
    }


def reference(x, mem, w_in, conv_mix_w, hg_lower_bounds, hg_norm_w, w_mem_kv, w_branch, w_out,
              norm_mix_pre, norm_mix_post, norm_mem, norm_ffn_pre, norm_ffn_post,
              w_ffn_up, conv_ffn_w, conv_ffn_b, w_ffn_down):
    bsz, s, _ = x.shape
    dt = x.dtype
    lb_soft = jax.nn.softmax(hg_lower_bounds.astype(jnp.float32), axis=0)
    lb_all = jnp.cumsum(lb_soft, axis=0) - lb_soft[0]
    split_idx = list(np.cumsum(IN_SPLITS)[:-1])

    for l in range(DEPTH):
        h = rms_norm(x, norm_mix_pre[l])
        proj = jnp.einsum('bsd,dn->bsn', h, w_in[l])
        (cb, cc, cv, hq, hf, hi, hg, mq, ga, gb, gm) = jnp.split(proj, split_idx, axis=-1)

        y_a = cb * causal_dwconv3(cc * cv, conv_mix_w[l])

        q = jax.nn.silu(hq).reshape(bsz, s, HG_HEADS, HG_KEY_DIM) * (HG_KEY_DIM ** -0.5)
        lb = jnp.clip(lb_all[l], 0.0, 1.0).reshape(HG_HEADS, HG_KEY_DIM)
        zf = hf.astype(jnp.float32).reshape(bsz, s, HG_HEADS, HG_KEY_DIM)
        log_f = jnp.logaddexp(jnp.log(lb + LOG_TINY), jnp.log1p(-lb) + jax.nn.log_sigmoid(zf))
        k = -jnp.expm1(log_f)
        vin = hi.reshape(bsz, s, HG_HEADS, HG_VAL_DIM)
        o = hgrn2_chunked(q, k, vin, log_f)
        o = o * lax.rsqrt(jnp.mean(o * o, axis=-1, keepdims=True) + RMS_EPS)
        o = o * hg_norm_w[l].astype(jnp.float32).reshape(HG_HEADS, HG_VAL_DIM)
        y_b = (o.reshape(bsz, s, HG_VWIDTH) * jax.nn.silu(hg.astype(jnp.float32))).astype(dt)

        mh = rms_norm(mem, norm_mem[l])
        mk, mv = jnp.split(jnp.einsum('bmd,dn->bmn', mh, w_mem_kv[l]), 2, axis=-1)
        qh = mq.reshape(bsz, s, MEM_HEADS, MEM_HEAD_DIM)
        kh = mk.reshape(bsz, N_MEM, MEM_HEADS, MEM_HEAD_DIM)
        vh = mv.reshape(bsz, N_MEM, MEM_HEADS, MEM_HEAD_DIM)
        sc = jnp.einsum('bshd,bmhd->bhsm', qh, kh).astype(jnp.float32) * (MEM_HEAD_DIM ** -0.5)
        p = jax.nn.softmax(sc, axis=-1).astype(dt)
        y_m = jnp.einsum('bhsm,bmhd->bshd', p, vh).reshape(bsz, s, MEM_WIDTH)

        merged = (jax.nn.sigmoid(ga) * jnp.einsum('bsc,cd->bsd', y_a, w_branch[l, 0])
                  + jax.nn.sigmoid(gb) * jnp.einsum('bsc,cd->bsd', y_b, w_branch[l, 1])
                  + jax.nn.sigmoid(gm) * jnp.einsum('bsc,cd->bsd', y_m, w_branch[l, 2]))
        mix_out = jnp.einsum('bsd,de->bse', merged, w_out[l])
        x = x + rms_norm(mix_out, norm_mix_post[l])

        h2 = rms_norm(x, norm_ffn_pre[l])
        up_g, up_v = jnp.split(jnp.einsum('bsd,df->bsf', h2, w_ffn_up[l]), 2, axis=-1)
        act = jax.nn.gelu(causal_dwconv3(up_g, conv_ffn_w[l]) + conv_ffn_b[l], approximate=True) * up_v
        ffn_out = jnp.einsum('bsf,fd->bsd', act, w_ffn_down[l])
        x = x + rms_norm(ffn_out, norm_ffn_post[l])
    return x
```

```python
import functools

import numpy as np
import jax
import jax.numpy as jnp
from jax import lax
from jax.experimental import pallas as pl
from jax.experimental.pallas import tpu as pltpu

F32 = jnp.float32
BF16 = jnp.bfloat16

RMS_EPS = 1e-6
LOG_TINY = 1e-30
CONV_K = 3
HG_HEADS = 8
HG_DIM = 128
HG_CHUNK = 64
HG_LEVELS = 6
MEM_HEADS = 4
SUBLANES = 8
VMEM_LIMIT = 56 * 1024 * 1024


def _params(semantics):
    return pltpu.CompilerParams(dimension_semantics=semantics, vmem_limit_bytes=VMEM_LIMIT)


def _dot(a, b):
    return jnp.dot(a, b, preferred_element_type=F32)


def _dot_nt(a, b):
    return lax.dot_general(a, b, (((1,), (1,)), ((), ())), preferred_element_type=F32)


def _dot_tn(a, b):
    return lax.dot_general(a, b, (((0,), (0,)), ((), ())), preferred_element_type=F32)


def _rms_scale(x):
    return lax.rsqrt(jnp.mean(x * x, axis=-1, keepdims=True) + RMS_EPS)


def _sigmoid(x):
    return 1.0 / (1.0 + jnp.exp(-x))


def _store_normed(x_ref, g_ref, h_ref, rows):
    tm = x_ref.shape[0]

    def body(r, _):
        sl = pl.ds(pl.multiple_of(r * rows, rows), rows)
        x = x_ref[sl, :]
        h_ref[sl, :] = (x * _rms_scale(x) * g_ref[...]).astype(BF16)
        return 0

    lax.fori_loop(0, tm // rows, body, 0)


def _norm_proj_kernel(x_ref, g_ref, w_ref, o_ref, h_ref):
    @pl.when(pl.program_id(1) == 0)
    def _():
        _store_normed(x_ref, g_ref, h_ref, min(256, x_ref.shape[0]))

    o_ref[...] = _dot(h_ref[...], w_ref[...]).astype(o_ref.dtype)


def _norm_proj(x, g, w, tm, tn):
    s, d = x.shape
    n = w.shape[1]
    tm, tn = min(tm, s), min(tn, n)
    return pl.pallas_call(
        _norm_proj_kernel,
        grid=(s // tm, n // tn),
        in_specs=[pl.BlockSpec((tm, d), lambda i, j: (i, 0)),
                  pl.BlockSpec((1, d), lambda i, j: (0, 0)),
                  pl.BlockSpec((d, tn), lambda i, j: (0, j))],
        out_specs=pl.BlockSpec((tm, tn), lambda i, j: (i, j)),
        out_shape=jax.ShapeDtypeStruct((s, n), BF16),
        scratch_shapes=[pltpu.VMEM((tm, d), BF16)],
        compiler_params=_params(("arbitrary", "arbitrary")),
        name="norm_proj",
    )(x, g.reshape(1, d), w)


def _conv3_rows(u, halo, w):
    tm = u.shape[0]
    u1 = pltpu.roll(u, 1, 0)
    u2 = pltpu.roll(u, 2, 0)
    y = u2 * w[0:1, :] + u1 * w[1:2, :] + u * w[2:3, :]
    head = jnp.concatenate([halo, u[0:SUBLANES, :]], axis=0)
    h1 = pltpu.roll(head, 1, 0)[SUBLANES:, :]
    h2 = pltpu.roll(head, 2, 0)[SUBLANES:, :]
    y_head = h2 * w[0:1, :] + h1 * w[1:2, :] + u[0:SUBLANES, :] * w[2:3, :]
    if tm == SUBLANES:
        return y_head
    return jnp.concatenate([y_head, y[SUBLANES:, :]], axis=0)


def _conv_mix_kernel(b_ref, c_ref, v_ref, w_ref, o_ref, halo_ref):
    @pl.when(pl.program_id(0) == 0)
    def _():
        halo_ref[...] = jnp.zeros_like(halo_ref)

    tm = b_ref.shape[0]
    u = c_ref[...].astype(F32) * v_ref[...].astype(F32)
    y = _conv3_rows(u, halo_ref[...], w_ref[...])
    halo_ref[...] = u[tm - SUBLANES:, :]
    o_ref[...] = (b_ref[...].astype(F32) * y).astype(o_ref.dtype)


def _conv_mix(proj, w, tm):
    s = proj.shape[0]
    c = w.shape[1]
    tm = min(tm, s)
    return pl.pallas_call(
        _conv_mix_kernel,
        grid=(s // tm,),
        in_specs=[pl.BlockSpec((tm, c), lambda i: (i, 0)),
                  pl.BlockSpec((tm, c), lambda i: (i, 1)),
                  pl.BlockSpec((tm, c), lambda i: (i, 2)),
                  pl.BlockSpec((CONV_K, c), lambda i: (0, 0))],
        out_specs=pl.BlockSpec((tm, c), lambda i: (i, 0)),
        out_shape=jax.ShapeDtypeStruct((s, c), BF16),
        scratch_shapes=[pltpu.VMEM((SUBLANES, c), F32)],
        compiler_params=_params(("arbitrary",)),
        name="conv_mix",
    )(proj, proj, proj, w)


def _hgrn_masks():
    i = np.arange(HG_CHUNK)[:, None]
    j = np.arange(HG_CHUNK)[None, :]
    masks = [(i == j)]
    for lvl in range(1, HG_LEVELS + 1):
        same = (i >> lvl) == (j >> lvl)
        masks.append(same & (((i >> (lvl - 1)) & 1) == 1) & (((j >> (lvl - 1)) & 1) == 0))
    return np.stack(masks).astype(np.float32)


def _block_mid_rows(b, lvl):
    size = 1 << lvl
    half = size >> 1
    if half >= SUBLANES:
        parts = []
        for blk in range(HG_CHUNK // size):
            r = blk * size + half - 1
            parts.append(jnp.broadcast_to(b[r:r + 1, :], (size, HG_DIM)))
        return parts[0] if len(parts) == 1 else jnp.concatenate(parts, axis=0)
    if size == 2:
        row = lax.broadcasted_iota(jnp.int32, b.shape, 0)
        return jnp.where((row & 1) == 1, pltpu.roll(b, 1, 0), b)
    groups = HG_CHUNK // SUBLANES
    b3 = b.reshape(groups, SUBLANES, HG_DIM)
    shape3 = (groups, SUBLANES, HG_DIM)
    if size == 8:
        out = jnp.broadcast_to(b3[:, 3:4, :], shape3)
    else:
        sub = lax.broadcasted_iota(jnp.int32, shape3, 1)
        out = jnp.where(sub < 4, jnp.broadcast_to(b3[:, 1:2, :], shape3),
                        jnp.broadcast_to(b3[:, 5:6, :], shape3))
    return out.reshape(HG_CHUNK, HG_DIM)


def _hgrn_kernel(layer, q_ref, f_ref, i_ref, g_ref, lb_ref, nw_ref, tri_ref, mask_ref,
                 o_ref, st_ref):
    @pl.when(pl.program_id(1) == 0)
    def _():
        st_ref[...] = jnp.zeros_like(st_ref)

    lbs = lb_ref[...]
    e = jnp.exp(lbs - jnp.max(lbs, axis=0, keepdims=True))
    soft = e / jnp.sum(e, axis=0, keepdims=True)
    lb = jnp.zeros((1, HG_DIM), F32)
    for r in range(1, layer + 1):
        lb = lb + soft[r:r + 1, :]
    lb = jnp.clip(lb, 0.0, 1.0)
    one_minus_lb = 1.0 - lb
    lb_tiny = lb + LOG_TINY
    nw = nw_ref[...]
    tri = tri_ref[...]

    def chunk(c, _):
        rows = pl.ds(pl.multiple_of(c * HG_CHUNK, HG_CHUNK), HG_CHUNK)
        zq = q_ref[rows, :].astype(F32)
        zf = f_ref[rows, :].astype(F32)
        v = i_ref[rows, :]
        zg = g_ref[rows, :].astype(F32)

        q = zq * _sigmoid(zq) * (HG_DIM ** -0.5)
        sig = _sigmoid(zf)
        logf = jnp.log(lb_tiny + one_minus_lb * sig)
        k = one_minus_lb * (1.0 - sig) - LOG_TINY

        p0 = logf.astype(BF16)
        r0 = logf - p0.astype(F32)
        p1 = r0.astype(BF16)
        p2 = (r0 - p1.astype(F32)).astype(BF16)
        cum3 = _dot(tri, jnp.concatenate([p0, p1, p2], axis=1))
        b = cum3[:, :HG_DIM] + cum3[:, HG_DIM:2 * HG_DIM] + cum3[:, 2 * HG_DIM:]
        b_last = b[HG_CHUNK - 1:HG_CHUNK, :]

        scores = mask_ref[0] * _dot_nt(q.astype(BF16), k.astype(BF16))
        for lvl in range(1, HG_LEVELS + 1):
            w = jnp.exp(-jnp.abs(b - _block_mid_rows(b, lvl)))
            scores = scores + mask_ref[lvl] * _dot_nt((q * w).astype(BF16), (k * w).astype(BF16))

        st = st_ref[...]
        o = _dot(scores.astype(BF16), v) + _dot_nt((q * jnp.exp(b)).astype(BF16), st.astype(BF16))
        k_end = (k * jnp.exp(b_last - b)).astype(BF16)
        st_ref[...] = st * jnp.exp(b_last) + _dot_tn(v, k_end)

        o = o * _rms_scale(o) * nw
        o_ref[rows, :] = (o * (zg * _sigmoid(zg))).astype(o_ref.dtype)
        return 0

    lax.fori_loop(0, q_ref.shape[0] // HG_CHUNK, chunk, 0)


def _hgrn(proj, lower_bounds, norm_w, layer, tm, col0):
    s = proj.shape[0]
    tm = min(tm, s)
    depth = lower_bounds.shape[0]
    width = HG_HEADS * HG_DIM
    cb = col0 // HG_DIM
    tri = jnp.asarray(np.tril(np.ones((HG_CHUNK, HG_CHUNK), np.float32)), BF16)
    masks = jnp.asarray(_hgrn_masks())

    def col(k):
        return pl.BlockSpec((tm, HG_DIM), lambda h, i: (i, cb + k * HG_HEADS + h))

    return pl.pallas_call(
        functools.partial(_hgrn_kernel, layer),
        grid=(HG_HEADS, s // tm),
        in_specs=[col(0), col(1), col(2), col(3),
                  pl.BlockSpec((depth, HG_DIM), lambda h, i: (0, h)),
                  pl.BlockSpec((1, HG_DIM), lambda h, i: (0, h)),
                  pl.BlockSpec((HG_CHUNK, HG_CHUNK), lambda h, i: (0, 0)),
                  pl.BlockSpec((HG_LEVELS + 1, HG_CHUNK, HG_CHUNK), lambda h, i: (0, 0, 0))],
        out_specs=pl.BlockSpec((tm, HG_DIM), lambda h, i: (i, h)),
        out_shape=jax.ShapeDtypeStruct((s, width), BF16),
        scratch_shapes=[pltpu.VMEM((HG_DIM, HG_DIM), F32)],
        compiler_params=_params(("arbitrary", "arbitrary")),
        name="hgrn2",
    )(proj, proj, proj, proj, lower_bounds, norm_w.reshape(1, width), tri, masks)


def _mem_attn_kernel(q_ref, kv_ref, o_ref):
    width = q_ref.shape[1]
    hd = width // MEM_HEADS
    for h in range(MEM_HEADS):
        q = q_ref[:, h * hd:(h + 1) * hd]
        k = kv_ref[:, h * hd:(h + 1) * hd]
        v = kv_ref[:, width + h * hd:width + (h + 1) * hd]
        sc = _dot_nt(q, k) * (hd ** -0.5)
        p = jnp.exp(sc - jnp.max(sc, axis=-1, keepdims=True))
        p = p / jnp.sum(p, axis=-1, keepdims=True)
        o_ref[:, h * hd:(h + 1) * hd] = _dot(p.astype(BF16), v).astype(o_ref.dtype)


def _mem_attn(proj, mem_kv, tm, col0):
    s = proj.shape[0]
    tm = min(tm, s)
    n_mem, kv_width = mem_kv.shape
    width = kv_width // 2
    return pl.pallas_call(
        _mem_attn_kernel,
        grid=(s // tm,),
        in_specs=[pl.BlockSpec((tm, width), lambda i: (i, col0 // width)),
                  pl.BlockSpec((n_mem, kv_width), lambda i: (0, 0))],
        out_specs=pl.BlockSpec((tm, width), lambda i: (i, 0)),
        out_shape=jax.ShapeDtypeStruct((s, width), BF16),
        compiler_params=_params(("arbitrary",)),
        name="mem_attn",
    )(proj, mem_kv)


def _merge_kernel(ya_ref, yb_ref, ym_ref, ga_ref, gb_ref, gm_ref, w_ref, o_ref):
    acc = _sigmoid(ga_ref[...].astype(F32)) * _dot(ya_ref[...], w_ref[0])
    acc = acc + _sigmoid(gb_ref[...].astype(F32)) * _dot(yb_ref[...], w_ref[1])
    acc = acc + _sigmoid(gm_ref[...].astype(F32)) * _dot(ym_ref[...], w_ref[2])
    o_ref[...] = acc.astype(o_ref.dtype)


def _merge(y_a, y_b, y_m, proj, w_branch, tm, tn, col0):
    s, c = y_a.shape
    d = w_branch.shape[2]
    tm, tn = min(tm, s), min(tn, d)
    gcb = col0 // tn
    nj = d // tn

    def gate(k):
        return pl.BlockSpec((tm, tn), lambda j, i: (i, gcb + k * nj + j))

    y_spec = pl.BlockSpec((tm, c), lambda j, i: (i, 0))
    return pl.pallas_call(
        _merge_kernel,
        grid=(nj, s // tm),
        in_specs=[y_spec, y_spec, y_spec, gate(0), gate(1), gate(2),
                  pl.BlockSpec((3, c, tn), lambda j, i: (0, 0, j))],
        out_specs=pl.BlockSpec((tm, tn), lambda j, i: (i, j)),
        out_shape=jax.ShapeDtypeStruct((s, d), BF16),
        compiler_params=_params(("arbitrary", "arbitrary")),
        name="merge",
    )(y_a, y_b, y_m, proj, proj, proj, w_branch)


def _proj_norm_res_kernel(a_ref, w_ref, x_ref, g_ref, o_ref, acc_ref):
    kk = pl.program_id(1)

    @pl.when(kk == 0)
    def _():
        acc_ref[...] = jnp.zeros_like(acc_ref)

    acc_ref[...] += _dot(a_ref[...], w_ref[...])

    @pl.when(kk == pl.num_programs(1) - 1)
    def _():
        y = acc_ref[...]
        o_ref[...] = x_ref[...] + y * _rms_scale(y) * g_ref[...]


def _proj_norm_res(a, w, x, g, tm, tk):
    s, k = a.shape
    d = w.shape[1]
    tm, tk = min(tm, s), min(tk, k)
    return pl.pallas_call(
        _proj_norm_res_kernel,
        grid=(s // tm, k // tk),
        in_specs=[pl.BlockSpec((tm, tk), lambda i, kk: (i, kk)),
                  pl.BlockSpec((tk, d), lambda i, kk: (kk, 0)),
                  pl.BlockSpec((tm, d), lambda i, kk: (i, 0)),
                  pl.BlockSpec((1, d), lambda i, kk: (0, 0))],
        out_specs=pl.BlockSpec((tm, d), lambda i, kk: (i, 0)),
        out_shape=jax.ShapeDtypeStruct((s, d), F32),
        scratch_shapes=[pltpu.VMEM((tm, d), F32)],
        compiler_params=_params(("arbitrary", "arbitrary")),
        name="proj_norm_res",
    )(a, w, x, g.reshape(1, d))


def _gelu_tanh(x):
    return 0.5 * x * (1.0 + jnp.tanh(np.sqrt(2.0 / np.pi) * (x + 0.044715 * (x * x * x))))


def _ffn_up_kernel(x_ref, g_ref, wg_ref, wv_ref, cw_ref, cb_ref, o_ref, h_ref, halo_ref):
    i = pl.program_id(0)
    j = pl.program_id(1)

    @pl.when(j == 0)
    def _():
        _store_normed(x_ref, g_ref, h_ref, min(256, x_ref.shape[0]))

    @pl.when(i == 0)
    def _():
        halo_ref[j] = jnp.zeros(halo_ref.shape[1:], F32)

    tm = x_ref.shape[0]
    h = h_ref[...]
    up_g = _dot(h, wg_ref[...])
    z = _conv3_rows(up_g, halo_ref[j], cw_ref[...]) + cb_ref[...]
    halo_ref[j] = up_g[tm - SUBLANES:, :]
    o_ref[...] = (_gelu_tanh(z) * _dot(h, wv_ref[...])).astype(o_ref.dtype)


def _ffn_up(x, g, w_up, conv_w, conv_b, tm, tn):
    s, d = x.shape
    f = conv_w.shape[1]
    tm, tn = min(tm, s), min(tn, f)
    nj = f // tn
    return pl.pallas_call(
        _ffn_up_kernel,
        grid=(s // tm, nj),
        in_specs=[pl.BlockSpec((tm, d), lambda i, j: (i, 0)),
                  pl.BlockSpec((1, d), lambda i, j: (0, 0)),
                  pl.BlockSpec((d, tn), lambda i, j: (0, j)),
                  pl.BlockSpec((d, tn), lambda i, j: (0, nj + j)),
                  pl.BlockSpec((CONV_K, tn), lambda i, j: (0, j)),
                  pl.BlockSpec((1, tn), lambda i, j: (0, j))],
        out_specs=pl.BlockSpec((tm, tn), lambda i, j: (i, j)),
        out_shape=jax.ShapeDtypeStruct((s, f), BF16),
        scratch_shapes=[pltpu.VMEM((tm, d), BF16), pltpu.VMEM((nj, SUBLANES, tn), F32)],
        compiler_params=_params(("arbitrary", "arbitrary")),
        name="ffn_up",
    )(x, g.reshape(1, d), w_up, w_up, conv_w, conv_b.reshape(1, f))


def kernel(x, mem, w_in, conv_mix_w, hg_lower_bounds, hg_norm_w, w_mem_kv, w_branch, w_out,
           norm_mix_pre, norm_mix_post, norm_mem, norm_ffn_pre, norm_ffn_post,
           w_ffn_up, conv_ffn_w, conv_ffn_b, w_ffn_down):
    bsz, s, d = x.shape
    depth = w_in.shape[0]
    conv_width = conv_mix_w.shape[2]
    hg_width = hg_norm_w.shape[1]
    mem_width = w_mem_kv.shape[2] // 2
    hg_col0 = 3 * conv_width
    mq_col0 = hg_col0 + 4 * hg_width
    gate_col0 = mq_col0 + mem_width
    lower_bounds = hg_lower_bounds.astype(F32)

    outs = []
    for b in range(bsz):
        xb = x[b]
        mb = mem[b]
        for l in range(depth):
            proj = _norm_proj(xb, norm_mix_pre[l], w_in[l].astype(BF16), 1024, 1024)
            mem_kv = _norm_proj(mb, norm_mem[l], w_mem_kv[l].astype(BF16), 256, 1024)
            y_a = _conv_mix(proj, conv_mix_w[l], 512)
            y_b = _hgrn(proj, lower_bounds, hg_norm_w[l], l, 1024, hg_col0)
            y_m = _mem_attn(proj, mem_kv, 512, mq_col0)
            merged = _merge(y_a, y_b, y_m, proj, w_branch[l].astype(BF16), 512, 1024, gate_col0)
            xb = _proj_norm_res(merged, w_out[l].astype(BF16), xb, norm_mix_post[l], 512, d)
            act = _ffn_up(xb, norm_ffn_pre[l], w_ffn_up[l].astype(BF16), conv_ffn_w[l],
                          conv_ffn_b[l], 1024, 512)
            xb = _proj_norm_res(act, w_ffn_down[l].astype(BF16), xb, norm_ffn_post[l], 512, 1408)
        outs.append(xb)
    return jnp.stack(outs)
```

```python
import functools

import numpy as np
import jax
import jax.numpy as jnp
from jax import lax
from jax.experimental import pallas as pl
from jax.experimental.pallas import tpu as pltpu

F32 = jnp.float32
BF16 = jnp.bfloat16

RMS_EPS = 1e-6
LOG_TINY = 1e-30
CONV_K = 3
HG_HEADS = 8
HG_DIM = 128
HG_CHUNK = 64
HG_LEVELS = 6
HG_GROUP = 8
LOG2E = 1.4426950408889634
MEM_HEADS = 4
SUBLANES = 8
VMEM_LIMIT = 56 * 1024 * 1024


def _params(semantics):
    return pltpu.CompilerParams(dimension_semantics=semantics, vmem_limit_bytes=VMEM_LIMIT)


def _dot(a, b):
    return jnp.dot(a, b, preferred_element_type=F32)


def _dot_nt(a, b):
    return lax.dot_general(a, b, (((1,), (1,)), ((), ())), preferred_element_type=F32)


def _dot_tn(a, b):
    return lax.dot_general(a, b, (((0,), (0,)), ((), ())), preferred_element_type=F32)


def _rms_scale(x):
    return lax.rsqrt(jnp.mean(x * x, axis=-1, keepdims=True) + RMS_EPS)


def _sigmoid(x):
    return 1.0 / (1.0 + jnp.exp(-x))


def _store_normed(x_ref, g_ref, h_ref, rows):
    tm = x_ref.shape[0]

    def body(r, _):
        sl = pl.ds(pl.multiple_of(r * rows, rows), rows)
        x = x_ref[sl, :]
        h_ref[sl, :] = (x * _rms_scale(x) * g_ref[...]).astype(BF16)
        return 0

    lax.fori_loop(0, tm // rows, body, 0)


def _norm_proj_kernel(x_ref, g_ref, w_ref, o_ref, h_ref):
    @pl.when(pl.program_id(1) == 0)
    def _():
        _store_normed(x_ref, g_ref, h_ref, min(256, x_ref.shape[0]))

    o_ref[...] = _dot(h_ref[...], w_ref[...]).astype(o_ref.dtype)


def _norm_proj(x, g, w, layer, tm, tn):
    s, d = x.shape
    n = w.shape[2]
    tm, tn = min(tm, s), min(tn, n)
    return pl.pallas_call(
        _norm_proj_kernel,
        grid=(s // tm, n // tn),
        in_specs=[pl.BlockSpec((tm, d), lambda i, j: (i, 0)),
                  pl.BlockSpec((None, 1, d), lambda i, j: (layer, 0, 0)),
                  pl.BlockSpec((None, d, tn), lambda i, j: (layer, 0, j))],
        out_specs=pl.BlockSpec((tm, tn), lambda i, j: (i, j)),
        out_shape=jax.ShapeDtypeStruct((s, n), BF16),
        scratch_shapes=[pltpu.VMEM((tm, d), BF16)],
        compiler_params=_params(("arbitrary", "arbitrary")),
        name="norm_proj",
    )(x, g, w)


def _conv3_rows(u, halo, w):
    tm = u.shape[0]
    u1 = pltpu.roll(u, 1, 0)
    u2 = pltpu.roll(u, 2, 0)
    y = u2 * w[0:1, :] + u1 * w[1:2, :] + u * w[2:3, :]
    head = jnp.concatenate([halo, u[0:SUBLANES, :]], axis=0)
    h1 = pltpu.roll(head, 1, 0)[SUBLANES:, :]
    h2 = pltpu.roll(head, 2, 0)[SUBLANES:, :]
    y_head = h2 * w[0:1, :] + h1 * w[1:2, :] + u[0:SUBLANES, :] * w[2:3, :]
    if tm == SUBLANES:
        return y_head
    return jnp.concatenate([y_head, y[SUBLANES:, :]], axis=0)


def _conv_mix_kernel(b_ref, c_ref, v_ref, w_ref, o_ref, halo_ref):
    @pl.when(pl.program_id(0) == 0)
    def _():
        halo_ref[...] = jnp.zeros_like(halo_ref)

    tm = b_ref.shape[0]
    u = c_ref[...].astype(F32) * v_ref[...].astype(F32)
    y = _conv3_rows(u, halo_ref[...], w_ref[...])
    halo_ref[...] = u[tm - SUBLANES:, :]
    o_ref[...] = (b_ref[...].astype(F32) * y).astype(o_ref.dtype)


def _conv_mix(proj, w, layer, tm):
    s = proj.shape[0]
    c = w.shape[2]
    tm = min(tm, s)
    return pl.pallas_call(
        _conv_mix_kernel,
        grid=(s // tm,),
        in_specs=[pl.BlockSpec((tm, c), lambda i: (i, 0)),
                  pl.BlockSpec((tm, c), lambda i: (i, 1)),
                  pl.BlockSpec((tm, c), lambda i: (i, 2)),
                  pl.BlockSpec((None, CONV_K, c), lambda i: (layer, 0, 0))],
        out_specs=pl.BlockSpec((tm, c), lambda i: (i, 0)),
        out_shape=jax.ShapeDtypeStruct((s, c), BF16),
        scratch_shapes=[pltpu.VMEM((SUBLANES, c), F32)],
        compiler_params=_params(("arbitrary",)),
        name="conv_mix",
    )(proj, proj, proj, w)


def _hgrn_pair_levels():
    i = np.arange(HG_CHUNK)[:, None]
    j = np.arange(HG_CHUNK)[None, :]
    lvl = np.full((HG_CHUNK, HG_CHUNK), HG_LEVELS + 1, np.int32)
    lvl[i == j] = 0
    for level in range(1, HG_LEVELS + 1):
        lvl[((i >> level) == (j >> level)) & (((i >> (level - 1)) & 1) == 1)
            & (((j >> (level - 1)) & 1) == 0)] = level
    return lvl


def _level_operand(q, k, b, lvl):
    size = 1 << lvl
    half = size >> 1
    if half >= SUBLANES:
        parts = []
        for lo in range(0, HG_CHUNK, size):
            mid, hi = lo + half, lo + size
            bm = b[mid - 1:mid, :]
            parts.append(k[lo:mid, :] * jnp.exp(bm - b[lo:mid, :]))
            parts.append(q[mid:hi, :] * jnp.exp(b[mid:hi, :] - bm))
        return jnp.concatenate(parts, axis=0).astype(BF16)
    groups = HG_CHUNK // SUBLANES
    shape3 = (groups, SUBLANES, HG_DIM)
    b3, q3, k3 = b.reshape(shape3), q.reshape(shape3), k.reshape(shape3)
    sub = lax.broadcasted_iota(jnp.int32, (1, SUBLANES, HG_DIM), 1)
    upper = ((sub >> (lvl - 1)) & 1) == 1
    if size == 2:
        bm = pltpu.roll(b, 1, 0).reshape(shape3)
        x = jnp.where(upper, (b3 - bm) * LOG2E, 0.0)
    else:
        if size == 8:
            bm = jnp.broadcast_to(b3[:, 3:4, :], shape3)
        else:
            bm = jnp.where(sub < 4, jnp.broadcast_to(b3[:, 1:2, :], shape3),
                           jnp.broadcast_to(b3[:, 5:6, :], shape3))
        x = (b3 - bm) * jnp.where(upper, LOG2E, -LOG2E)
    a = jnp.where(upper, q3, k3) * jnp.exp2(x)
    return a.reshape(HG_CHUNK, HG_DIM).astype(BF16)


def _hgrn_kernel(layer, q_ref, f_ref, i_ref, g_ref, lb_ref, nw_ref, tri_ref, lvl_ref,
                 o_ref, st_ref):
    @pl.when(pl.program_id(1) == 0)
    def _():
        st_ref[...] = jnp.zeros_like(st_ref)

    lbs = lb_ref[...]
    e = jnp.exp(lbs - jnp.max(lbs, axis=0, keepdims=True))
    soft = e / jnp.sum(e, axis=0, keepdims=True)
    lb = jnp.zeros((1, HG_DIM), F32)
    for r in range(1, layer + 1):
        lb = lb + soft[r:r + 1, :]
    lb = jnp.clip(lb, 0.0, 1.0)
    one_minus_lb = 1.0 - lb
    lb_tiny = lb + LOG_TINY
    nw = nw_ref[...]
    tri = tri_ref[...]
    pair_level = lvl_ref[...]
    chunks = range(HG_GROUP)

    def group(gi, _):
        rows = [pl.ds(pl.multiple_of((gi * HG_GROUP + c) * HG_CHUNK, HG_CHUNK), HG_CHUNK)
                for c in chunks]

        q, k, b = [], [], []
        for c in chunks:
            zq = q_ref[rows[c], :].astype(F32)
            zf = f_ref[rows[c], :].astype(F32)
            q.append(zq * _sigmoid(zq) * (HG_DIM ** -0.5))
            sig = _sigmoid(zf)
            logf = jnp.log(lb_tiny + one_minus_lb * sig)
            k.append(one_minus_lb * (1.0 - sig) - LOG_TINY)
            p0 = logf.astype(BF16)
            r0 = logf - p0.astype(F32)
            p1 = r0.astype(BF16)
            p2 = (r0 - p1.astype(F32)).astype(BF16)
            cum3 = _dot(tri, jnp.concatenate([p0, p1, p2], axis=1))
            b.append(cum3[:, :HG_DIM] + cum3[:, HG_DIM:2 * HG_DIM] + cum3[:, 2 * HG_DIM:])

        scores = []
        for c in chunks:
            s = jnp.where(pair_level == 0, _dot_nt(q[c].astype(BF16), k[c].astype(BF16)), 0.0)
            for lvl in range(1, HG_LEVELS + 1):
                a = _level_operand(q[c], k[c], b[c], lvl)
                s = jnp.where(pair_level == lvl, _dot_nt(a, a), s)
            scores.append(s.astype(BF16))

        v = [i_ref[rows[c], :] for c in chunks]
        b_last = [b[c][HG_CHUNK - 1:HG_CHUNK, :] for c in chunks]
        kv = [_dot_tn(v[c], (k[c] * jnp.exp(b_last[c] - b[c])).astype(BF16)) for c in chunks]
        st = st_ref[...]
        o_inter = []
        for c in chunks:
            o_inter.append(_dot_nt((q[c] * jnp.exp(b[c])).astype(BF16), st.astype(BF16)))
            st = st * jnp.exp(b_last[c]) + kv[c]
        st_ref[...] = st

        for c in chunks:
            o = _dot(scores[c], v[c]) + o_inter[c]
            zg = g_ref[rows[c], :].astype(F32)
            o = o * _rms_scale(o) * nw
            o_ref[rows[c], :] = (o * (zg * _sigmoid(zg))).astype(o_ref.dtype)
        return 0

    lax.fori_loop(0, q_ref.shape[0] // (HG_CHUNK * HG_GROUP), group, 0)


def _hgrn(proj, lower_bounds, norm_w, layer, tm, col0):
    s = proj.shape[0]
    tm = min(tm, s)
    assert tm % (HG_CHUNK * HG_GROUP) == 0
    depth = lower_bounds.shape[0]
    width = HG_HEADS * HG_DIM
    cb = col0 // HG_DIM
    tri = jnp.asarray(np.tril(np.ones((HG_CHUNK, HG_CHUNK), np.float32)), BF16)
    pair_levels = jnp.asarray(_hgrn_pair_levels())

    def col(k):
        return pl.BlockSpec((tm, HG_DIM), lambda h, i: (i, cb + k * HG_HEADS + h))

    return pl.pallas_call(
        functools.partial(_hgrn_kernel, layer),
        grid=(HG_HEADS, s // tm),
        in_specs=[col(0), col(1), col(2), col(3),
                  pl.BlockSpec((depth, HG_DIM), lambda h, i: (0, h)),
                  pl.BlockSpec((None, 1, HG_DIM), lambda h, i: (layer, 0, h)),
                  pl.BlockSpec((HG_CHUNK, HG_CHUNK), lambda h, i: (0, 0)),
                  pl.BlockSpec((HG_CHUNK, HG_CHUNK), lambda h, i: (0, 0))],
        out_specs=pl.BlockSpec((tm, HG_DIM), lambda h, i: (i, h)),
        out_shape=jax.ShapeDtypeStruct((s, width), BF16),
        scratch_shapes=[pltpu.VMEM((HG_DIM, HG_DIM), F32)],
        compiler_params=_params(("arbitrary", "arbitrary")),
        name="hgrn2",
    )(proj, proj, proj, proj, lower_bounds, norm_w, tri, pair_levels)


def _mem_attn_kernel(q_ref, kv_ref, o_ref):
    width = q_ref.shape[1]
    hd = width // MEM_HEADS
    for h in range(MEM_HEADS):
        q = q_ref[:, h * hd:(h + 1) * hd]
        k = kv_ref[:, h * hd:(h + 1) * hd]
        v = kv_ref[:, width + h * hd:width + (h + 1) * hd]
        sc = _dot_nt(q, k) * (hd ** -0.5)
        p = jnp.exp(sc - jnp.max(sc, axis=-1, keepdims=True))
        p = p / jnp.sum(p, axis=-1, keepdims=True)
        o_ref[:, h * hd:(h + 1) * hd] = _dot(p.astype(BF16), v).astype(o_ref.dtype)


def _mem_attn(proj, mem_kv, tm, col0):
    s = proj.shape[0]
    tm = min(tm, s)
    n_mem, kv_width = mem_kv.shape
    width = kv_width // 2
    return pl.pallas_call(
        _mem_attn_kernel,
        grid=(s // tm,),
        in_specs=[pl.BlockSpec((tm, width), lambda i: (i, col0 // width)),
                  pl.BlockSpec((n_mem, kv_width), lambda i: (0, 0))],
        out_specs=pl.BlockSpec((tm, width), lambda i: (i, 0)),
        out_shape=jax.ShapeDtypeStruct((s, width), BF16),
        compiler_params=_params(("arbitrary",)),
        name="mem_attn",
    )(proj, mem_kv)


def _merge_kernel(ya_ref, yb_ref, ym_ref, ga_ref, gb_ref, gm_ref, w_ref, o_ref):
    acc = _sigmoid(ga_ref[...].astype(F32)) * _dot(ya_ref[...], w_ref[0])
    acc = acc + _sigmoid(gb_ref[...].astype(F32)) * _dot(yb_ref[...], w_ref[1])
    acc = acc + _sigmoid(gm_ref[...].astype(F32)) * _dot(ym_ref[...], w_ref[2])
    o_ref[...] = acc.astype(o_ref.dtype)


def _merge(y_a, y_b, y_m, proj, w_branch, layer, tm, tn, col0):
    s, c = y_a.shape
    d = w_branch.shape[3]
    tm, tn = min(tm, s), min(tn, d)
    gcb = col0 // tn
    nj = d // tn

    def gate(k):
        return pl.BlockSpec((tm, tn), lambda j, i: (i, gcb + k * nj + j))

    y_spec = pl.BlockSpec((tm, c), lambda j, i: (i, 0))
    return pl.pallas_call(
        _merge_kernel,
        grid=(nj, s // tm),
        in_specs=[y_spec, y_spec, y_spec, gate(0), gate(1), gate(2),
                  pl.BlockSpec((None, 3, c, tn), lambda j, i: (layer, 0, 0, j))],
        out_specs=pl.BlockSpec((tm, tn), lambda j, i: (i, j)),
        out_shape=jax.ShapeDtypeStruct((s, d), BF16),
        compiler_params=_params(("arbitrary", "arbitrary")),
        name="merge",
    )(y_a, y_b, y_m, proj, proj, proj, w_branch)


def _proj_norm_res_kernel(a_ref, w_ref, x_ref, g_ref, o_ref, acc_ref):
    kk = pl.program_id(1)

    @pl.when(kk == 0)
    def _():
        acc_ref[...] = jnp.zeros_like(acc_ref)

    acc_ref[...] += _dot(a_ref[...], w_ref[...])

    @pl.when(kk == pl.num_programs(1) - 1)
    def _():
        y = acc_ref[...]
        o_ref[...] = x_ref[...] + y * _rms_scale(y) * g_ref[...]


def _proj_norm_res(a, w, x, g, layer, tm, tk):
    s, k = a.shape
    d = w.shape[2]
    tm, tk = min(tm, s), min(tk, k)
    return pl.pallas_call(
        _proj_norm_res_kernel,
        grid=(s // tm, k // tk),
        in_specs=[pl.BlockSpec((tm, tk), lambda i, kk: (i, kk)),
                  pl.BlockSpec((None, tk, d), lambda i, kk: (layer, kk, 0)),
                  pl.BlockSpec((tm, d), lambda i, kk: (i, 0)),
                  pl.BlockSpec((None, 1, d), lambda i, kk: (layer, 0, 0))],
        out_specs=pl.BlockSpec((tm, d), lambda i, kk: (i, 0)),
        out_shape=jax.ShapeDtypeStruct((s, d), F32),
        scratch_shapes=[pltpu.VMEM((tm, d), F32)],
        compiler_params=_params(("arbitrary", "arbitrary")),
        name="proj_norm_res",
    )(a, w, x, g)


def _gelu_tanh(x):
    return 0.5 * x * (1.0 + jnp.tanh(np.sqrt(2.0 / np.pi) * (x + 0.044715 * (x * x * x))))


def _ffn_up_kernel(x_ref, g_ref, wg_ref, wv_ref, cw_ref, cb_ref, o_ref, h_ref, halo_ref):
    i = pl.program_id(0)
    j = pl.program_id(1)

    @pl.when(j == 0)
    def _():
        _store_normed(x_ref, g_ref, h_ref, min(256, x_ref.shape[0]))

    @pl.when(i == 0)
    def _():
        halo_ref[j] = jnp.zeros(halo_ref.shape[1:], F32)

    tm = x_ref.shape[0]
    h = h_ref[...]
    up_g = _dot(h, wg_ref[...])
    z = _conv3_rows(up_g, halo_ref[j], cw_ref[...]) + cb_ref[...]
    halo_ref[j] = up_g[tm - SUBLANES:, :]
    o_ref[...] = (_gelu_tanh(z) * _dot(h, wv_ref[...])).astype(o_ref.dtype)


def _ffn_up(x, g, w_up, conv_w, conv_b, layer, tm, tn):
    s, d = x.shape
    f = conv_w.shape[2]
    tm, tn = min(tm, s), min(tn, f)
    nj = f // tn
    return pl.pallas_call(
        _ffn_up_kernel,
        grid=(s // tm, nj),
        in_specs=[pl.BlockSpec((tm, d), lambda i, j: (i, 0)),
                  pl.BlockSpec((None, 1, d), lambda i, j: (layer, 0, 0)),
                  pl.BlockSpec((None, d, tn), lambda i, j: (layer, 0, j)),
                  pl.BlockSpec((None, d, tn), lambda i, j: (layer, 0, nj + j)),
                  pl.BlockSpec((None, CONV_K, tn), lambda i, j: (layer, 0, j)),
                  pl.BlockSpec((None, 1, tn), lambda i, j: (layer, 0, j))],
        out_specs=pl.BlockSpec((tm, tn), lambda i, j: (i, j)),
        out_shape=jax.ShapeDtypeStruct((s, f), BF16),
        scratch_shapes=[pltpu.VMEM((tm, d), BF16), pltpu.VMEM((nj, SUBLANES, tn), F32)],
        compiler_params=_params(("arbitrary", "arbitrary")),
        name="ffn_up",
    )(x, g, w_up, w_up, conv_w, conv_b)


def kernel(x, mem, w_in, conv_mix_w, hg_lower_bounds, hg_norm_w, w_mem_kv, w_branch, w_out,
           norm_mix_pre, norm_mix_post, norm_mem, norm_ffn_pre, norm_ffn_post,
           w_ffn_up, conv_ffn_w, conv_ffn_b, w_ffn_down):
    bsz, s, d = x.shape
    depth = w_in.shape[0]
    conv_width = conv_mix_w.shape[2]
    hg_width = hg_norm_w.shape[1]
    mem_width = w_mem_kv.shape[2] // 2
    hg_col0 = 3 * conv_width
    mq_col0 = hg_col0 + 4 * hg_width
    gate_col0 = mq_col0 + mem_width
    lower_bounds = hg_lower_bounds.astype(F32)

    def rows(p):
        return p.reshape(depth, 1, p.shape[1])

    w_in_b, w_kv_b, w_br_b, w_out_b, w_up_b, w_dn_b = (
        w.astype(BF16) for w in (w_in, w_mem_kv, w_branch, w_out, w_ffn_up, w_ffn_down))
    g_mix_pre, g_mix_post, g_mem, g_ffn_pre, g_ffn_post, hg_nw, ffn_b = (
        rows(p) for p in (norm_mix_pre, norm_mix_post, norm_mem, norm_ffn_pre, norm_ffn_post,
                          hg_norm_w, conv_ffn_b))

    outs = []
    for b in range(bsz):
        xb = x[b]
        mb = mem[b]
        for l in range(depth):
            proj = _norm_proj(xb, g_mix_pre, w_in_b, l, 1024, 1024)
            mem_kv = _norm_proj(mb, g_mem, w_kv_b, l, 256, 1024)
            y_a = _conv_mix(proj, conv_mix_w, l, 512)
            y_b = _hgrn(proj, lower_bounds, hg_nw, l, 1024, hg_col0)
            y_m = _mem_attn(proj, mem_kv, 512, mq_col0)
            merged = _merge(y_a, y_b, y_m, proj, w_br_b, l, 512, 1024, gate_col0)
            xb = _proj_norm_res(merged, w_out_b, xb, g_mix_post, l, 512, d)
            act = _ffn_up(xb, g_ffn_pre, w_up_b, conv_ffn_w, ffn_b, l, 1024, 512)
            xb = _proj_norm_res(act, w_dn_b, xb, g_ffn_post, l, 512, 1408)
        outs.append(xb)
    return outs[0][None] if bsz == 1 else jnp.stack(outs)
```

```python
import functools

import numpy as np
import jax
import jax.numpy as jnp
from jax import lax
from jax.experimental import pallas as pl
from jax.experimental.pallas import tpu as pltpu

F32 = jnp.float32
BF16 = jnp.bfloat16

RMS_EPS = 1e-6
LOG_TINY = 1e-30
CONV_K = 3
HG_HEADS = 8
HG_DIM = 128
HG_CHUNK = 64
HG_LEVELS = 6
HG_GROUP = 8
LOG2E = 1.4426950408889634
MEM_HEADS = 4
SUBLANES = 8
VMEM_LIMIT = 56 * 1024 * 1024


def _params(semantics):
    return pltpu.CompilerParams(dimension_semantics=semantics, vmem_limit_bytes=VMEM_LIMIT)


def _dot(a, b):
    return jnp.dot(a, b, preferred_element_type=F32)


def _dot_nt(a, b):
    return lax.dot_general(a, b, (((1,), (1,)), ((), ())), preferred_element_type=F32)


def _dot_tn(a, b):
    return lax.dot_general(a, b, (((0,), (0,)), ((), ())), preferred_element_type=F32)


def _rms_scale(x):
    return lax.rsqrt(jnp.mean(x * x, axis=-1, keepdims=True) + RMS_EPS)


def _sigmoid(x):
    return 1.0 / (1.0 + jnp.exp(-x))


def _store_normed(x_ref, g_ref, h_ref, rows):
    tm = x_ref.shape[0]

    def body(r, _):
        sl = pl.ds(pl.multiple_of(r * rows, rows), rows)
        x = x_ref[sl, :]
        h_ref[sl, :] = (x * _rms_scale(x) * g_ref[...]).astype(BF16)
        return 0

    lax.fori_loop(0, tm // rows, body, 0)


def _norm_proj_kernel(x_ref, g_ref, w_ref, o_ref, h_ref):
    @pl.when(pl.program_id(1) == 0)
    def _():
        _store_normed(x_ref, g_ref, h_ref, min(256, x_ref.shape[0]))

    o_ref[...] = _dot(h_ref[...], w_ref[...].astype(BF16)).astype(o_ref.dtype)


def _norm_proj(x, g, w, layer, tm, tn):
    s, d = x.shape
    n = w.shape[2]
    tm, tn = min(tm, s), min(tn, n)
    return pl.pallas_call(
        _norm_proj_kernel,
        grid=(s // tm, n // tn),
        in_specs=[pl.BlockSpec((tm, d), lambda i, j: (i, 0)),
                  pl.BlockSpec((None, 1, d), lambda i, j: (layer, 0, 0)),
                  pl.BlockSpec((None, d, tn), lambda i, j: (layer, 0, j))],
        out_specs=pl.BlockSpec((tm, tn), lambda i, j: (i, j)),
        out_shape=jax.ShapeDtypeStruct((s, n), BF16),
        scratch_shapes=[pltpu.VMEM((tm, d), BF16)],
        compiler_params=_params(("arbitrary", "arbitrary")),
        name="norm_proj",
    )(x, g, w)


def _conv3_rows(u, halo, w):
    tm = u.shape[0]
    u1 = pltpu.roll(u, 1, 0)
    u2 = pltpu.roll(u, 2, 0)
    y = u2 * w[0:1, :] + u1 * w[1:2, :] + u * w[2:3, :]
    head = jnp.concatenate([halo, u[0:SUBLANES, :]], axis=0)
    h1 = pltpu.roll(head, 1, 0)[SUBLANES:, :]
    h2 = pltpu.roll(head, 2, 0)[SUBLANES:, :]
    y_head = h2 * w[0:1, :] + h1 * w[1:2, :] + u[0:SUBLANES, :] * w[2:3, :]
    if tm == SUBLANES:
        return y_head
    return jnp.concatenate([y_head, y[SUBLANES:, :]], axis=0)


def _conv_mix_kernel(b_ref, c_ref, v_ref, w_ref, o_ref, halo_ref):
    @pl.when(pl.program_id(0) == 0)
    def _():
        halo_ref[...] = jnp.zeros_like(halo_ref)

    tm = b_ref.shape[0]
    u = c_ref[...].astype(F32) * v_ref[...].astype(F32)
    y = _conv3_rows(u, halo_ref[...], w_ref[...])
    halo_ref[...] = u[tm - SUBLANES:, :]
    o_ref[...] = (b_ref[...].astype(F32) * y).astype(o_ref.dtype)


def _conv_mix(proj, w, layer, tm):
    s = proj.shape[0]
    c = w.shape[2]
    tm = min(tm, s)
    return pl.pallas_call(
        _conv_mix_kernel,
        grid=(s // tm,),
        in_specs=[pl.BlockSpec((tm, c), lambda i: (i, 0)),
                  pl.BlockSpec((tm, c), lambda i: (i, 1)),
                  pl.BlockSpec((tm, c), lambda i: (i, 2)),
                  pl.BlockSpec((None, CONV_K, c), lambda i: (layer, 0, 0))],
        out_specs=pl.BlockSpec((tm, c), lambda i: (i, 0)),
        out_shape=jax.ShapeDtypeStruct((s, c), BF16),
        scratch_shapes=[pltpu.VMEM((SUBLANES, c), F32)],
        compiler_params=_params(("arbitrary",)),
        name="conv_mix",
    )(proj, proj, proj, w)


def _hgrn_pair_levels():
    i = np.arange(HG_CHUNK)[:, None]
    j = np.arange(HG_CHUNK)[None, :]
    lvl = np.full((HG_CHUNK, HG_CHUNK), HG_LEVELS + 1, np.int32)
    lvl[i == j] = 0
    for level in range(1, HG_LEVELS + 1):
        lvl[((i >> level) == (j >> level)) & (((i >> (level - 1)) & 1) == 1)
            & (((j >> (level - 1)) & 1) == 0)] = level
    return lvl


def _level_operand(q, k, b, lvl):
    size = 1 << lvl
    half = size >> 1
    if half >= SUBLANES:
        parts = []
        for lo in range(0, HG_CHUNK, size):
            mid, hi = lo + half, lo + size
            bm = b[mid - 1:mid, :]
            parts.append(k[lo:mid, :] * jnp.exp(bm - b[lo:mid, :]))
            parts.append(q[mid:hi, :] * jnp.exp(b[mid:hi, :] - bm))
        return jnp.concatenate(parts, axis=0).astype(BF16)
    groups = HG_CHUNK // SUBLANES
    shape3 = (groups, SUBLANES, HG_DIM)
    b3, q3, k3 = b.reshape(shape3), q.reshape(shape3), k.reshape(shape3)
    sub = lax.broadcasted_iota(jnp.int32, (1, SUBLANES, HG_DIM), 1)
    upper = ((sub >> (lvl - 1)) & 1) == 1
    if size == 2:
        bm = pltpu.roll(b, 1, 0).reshape(shape3)
        x = jnp.where(upper, (b3 - bm) * LOG2E, 0.0)
    else:
        if size == 8:
            bm = jnp.broadcast_to(b3[:, 3:4, :], shape3)
        else:
            bm = jnp.where(sub < 4, jnp.broadcast_to(b3[:, 1:2, :], shape3),
                           jnp.broadcast_to(b3[:, 5:6, :], shape3))
        x = (b3 - bm) * jnp.where(upper, LOG2E, -LOG2E)
    a = jnp.where(upper, q3, k3) * jnp.exp2(x)
    return a.reshape(HG_CHUNK, HG_DIM).astype(BF16)


def _hgrn_kernel(layer, q_ref, f_ref, i_ref, g_ref, lb_ref, nw_ref, tri_ref, lvl_ref,
                 o_ref, st_ref):
    @pl.when(pl.program_id(1) == 0)
    def _():
        st_ref[...] = jnp.zeros_like(st_ref)

    lbs = lb_ref[...]
    e = jnp.exp(lbs - jnp.max(lbs, axis=0, keepdims=True))
    soft = e / jnp.sum(e, axis=0, keepdims=True)
    lb = jnp.zeros((1, HG_DIM), F32)
    for r in range(1, layer + 1):
        lb = lb + soft[r:r + 1, :]
    lb = jnp.clip(lb, 0.0, 1.0)
    one_minus_lb = 1.0 - lb
    lb_tiny = lb + LOG_TINY
    nw = nw_ref[...]
    tri = tri_ref[...]
    pair_level = lvl_ref[...]
    chunks = range(HG_GROUP)

    def group(gi, _):
        rows = [pl.ds(pl.multiple_of((gi * HG_GROUP + c) * HG_CHUNK, HG_CHUNK), HG_CHUNK)
                for c in chunks]

        q, k, b = [], [], []
        for c in chunks:
            zq = q_ref[rows[c], :].astype(F32)
            zf = f_ref[rows[c], :].astype(F32)
            q.append(zq * _sigmoid(zq) * (HG_DIM ** -0.5))
            sig = _sigmoid(zf)
            logf = jnp.log(lb_tiny + one_minus_lb * sig)
            k.append(one_minus_lb * (1.0 - sig) - LOG_TINY)
            p0 = logf.astype(BF16)
            r0 = logf - p0.astype(F32)
            p1 = r0.astype(BF16)
            p2 = (r0 - p1.astype(F32)).astype(BF16)
            cum3 = _dot(tri, jnp.concatenate([p0, p1, p2], axis=1))
            b.append(cum3[:, :HG_DIM] + cum3[:, HG_DIM:2 * HG_DIM] + cum3[:, 2 * HG_DIM:])

        scores = []
        for c in chunks:
            s = jnp.where(pair_level == 0, _dot_nt(q[c].astype(BF16), k[c].astype(BF16)), 0.0)
            for lvl in range(1, HG_LEVELS + 1):
                a = _level_operand(q[c], k[c], b[c], lvl)
                s = jnp.where(pair_level == lvl, _dot_nt(a, a), s)
            scores.append(s.astype(BF16))

        v = [i_ref[rows[c], :] for c in chunks]
        b_last = [b[c][HG_CHUNK - 1:HG_CHUNK, :] for c in chunks]
        kv = [_dot_tn(v[c], (k[c] * jnp.exp(b_last[c] - b[c])).astype(BF16)) for c in chunks]
        st = st_ref[...]
        o_inter = []
        for c in chunks:
            o_inter.append(_dot_nt((q[c] * jnp.exp(b[c])).astype(BF16), st.astype(BF16)))
            st = st * jnp.exp(b_last[c]) + kv[c]
        st_ref[...] = st

        for c in chunks:
            o = _dot(scores[c], v[c]) + o_inter[c]
            zg = g_ref[rows[c], :].astype(F32)
            o = o * _rms_scale(o) * nw
            o_ref[rows[c], :] = (o * (zg * _sigmoid(zg))).astype(o_ref.dtype)
        return 0

    lax.fori_loop(0, q_ref.shape[0] // (HG_CHUNK * HG_GROUP), group, 0)


def _hgrn(proj, lower_bounds, norm_w, layer, tm, col0):
    s = proj.shape[0]
    tm = min(tm, s)
    assert tm % (HG_CHUNK * HG_GROUP) == 0
    depth = lower_bounds.shape[0]
    width = HG_HEADS * HG_DIM
    cb = col0 // HG_DIM
    tri = jnp.asarray(np.tril(np.ones((HG_CHUNK, HG_CHUNK), np.float32)), BF16)
    pair_levels = jnp.asarray(_hgrn_pair_levels())

    def col(k):
        return pl.BlockSpec((tm, HG_DIM), lambda h, i: (i, cb + k * HG_HEADS + h))

    return pl.pallas_call(
        functools.partial(_hgrn_kernel, layer),
        grid=(HG_HEADS, s // tm),
        in_specs=[col(0), col(1), col(2), col(3),
                  pl.BlockSpec((depth, HG_DIM), lambda h, i: (0, h)),
                  pl.BlockSpec((None, 1, HG_DIM), lambda h, i: (layer, 0, h)),
                  pl.BlockSpec((HG_CHUNK, HG_CHUNK), lambda h, i: (0, 0)),
                  pl.BlockSpec((HG_CHUNK, HG_CHUNK), lambda h, i: (0, 0))],
        out_specs=pl.BlockSpec((tm, HG_DIM), lambda h, i: (i, h)),
        out_shape=jax.ShapeDtypeStruct((s, width), BF16),
        scratch_shapes=[pltpu.VMEM((HG_DIM, HG_DIM), F32)],
        compiler_params=_params(("arbitrary", "arbitrary")),
        name="hgrn2",
    )(proj, proj, proj, proj, lower_bounds, norm_w, tri, pair_levels)


def _mem_attn_kernel(q_ref, kv_ref, o_ref):
    width = q_ref.shape[1]
    hd = width // MEM_HEADS
    for h in range(MEM_HEADS):
        q = q_ref[:, h * hd:(h + 1) * hd]
        k = kv_ref[:, h * hd:(h + 1) * hd]
        v = kv_ref[:, width + h * hd:width + (h + 1) * hd]
        sc = _dot_nt(q, k) * (hd ** -0.5)
        p = jnp.exp(sc - jnp.max(sc, axis=-1, keepdims=True))
        p = p / jnp.sum(p, axis=-1, keepdims=True)
        o_ref[:, h * hd:(h + 1) * hd] = _dot(p.astype(BF16), v).astype(o_ref.dtype)


def _mem_attn(proj, mem_kv, tm, col0):
    s = proj.shape[0]
    tm = min(tm, s)
    n_mem, kv_width = mem_kv.shape
    width = kv_width // 2
    return pl.pallas_call(
        _mem_attn_kernel,
        grid=(s // tm,),
        in_specs=[pl.BlockSpec((tm, width), lambda i: (i, col0 // width)),
                  pl.BlockSpec((n_mem, kv_width), lambda i: (0, 0))],
        out_specs=pl.BlockSpec((tm, width), lambda i: (i, 0)),
        out_shape=jax.ShapeDtypeStruct((s, width), BF16),
        compiler_params=_params(("arbitrary",)),
        name="mem_attn",
    )(proj, mem_kv)


def _merge_kernel(ya_ref, yb_ref, ym_ref, ga_ref, gb_ref, gm_ref, w_ref, o_ref):
    acc = _sigmoid(ga_ref[...].astype(F32)) * _dot(ya_ref[...], w_ref[0])
    acc = acc + _sigmoid(gb_ref[...].astype(F32)) * _dot(yb_ref[...], w_ref[1])
    acc = acc + _sigmoid(gm_ref[...].astype(F32)) * _dot(ym_ref[...], w_ref[2])
    o_ref[...] = acc.astype(o_ref.dtype)


def _merge(y_a, y_b, y_m, proj, w_branch, layer, tm, tn, col0):
    s, c = y_a.shape
    d = w_branch.shape[3]
    tm, tn = min(tm, s), min(tn, d)
    gcb = col0 // tn
    nj = d // tn

    def gate(k):
        return pl.BlockSpec((tm, tn), lambda j, i: (i, gcb + k * nj + j))

    y_spec = pl.BlockSpec((tm, c), lambda j, i: (i, 0))
    return pl.pallas_call(
        _merge_kernel,
        grid=(nj, s // tm),
        in_specs=[y_spec, y_spec, y_spec, gate(0), gate(1), gate(2),
                  pl.BlockSpec((None, 3, c, tn), lambda j, i: (layer, 0, 0, j))],
        out_specs=pl.BlockSpec((tm, tn), lambda j, i: (i, j)),
        out_shape=jax.ShapeDtypeStruct((s, d), BF16),
        compiler_params=_params(("arbitrary", "arbitrary")),
        name="merge",
    )(y_a, y_b, y_m, proj, proj, proj, w_branch)


def _proj_norm_res_kernel(a_ref, w_ref, x_ref, g_ref, o_ref):
    kk = pl.program_id(1)

    @pl.when(kk == 0)
    def _():
        o_ref[...] = _dot(a_ref[...], w_ref[...])

    @pl.when(kk > 0)
    def _():
        o_ref[...] += _dot(a_ref[...], w_ref[...])

    @pl.when(kk == pl.num_programs(1) - 1)
    def _():
        rows = min(256, o_ref.shape[0])

        def body(r, _):
            sl = pl.ds(pl.multiple_of(r * rows, rows), rows)
            acc = o_ref[sl, :]
            o_ref[sl, :] = x_ref[sl, :] + acc * _rms_scale(acc) * g_ref[...]
            return 0

        lax.fori_loop(0, o_ref.shape[0] // rows, body, 0)


def _proj_norm_res(a, w, x, g, layer, tm, tk):
    s, k = a.shape
    d = w.shape[2]
    tm, tk = min(tm, s), min(tk, k)
    return pl.pallas_call(
        _proj_norm_res_kernel,
        grid=(s // tm, k // tk),
        in_specs=[pl.BlockSpec((tm, tk), lambda i, kk: (i, kk)),
                  pl.BlockSpec((None, tk, d), lambda i, kk: (layer, kk, 0)),
                  pl.BlockSpec((tm, d), lambda i, kk: (i, 0)),
                  pl.BlockSpec((None, 1, d), lambda i, kk: (layer, 0, 0))],
        out_specs=pl.BlockSpec((tm, d), lambda i, kk: (i, 0)),
        out_shape=jax.ShapeDtypeStruct((s, d), F32),
        compiler_params=_params(("arbitrary", "arbitrary")),
        name="proj_norm_res",
    )(a, w, x, g)


def _gelu_tanh(x):
    return 0.5 * x * (1.0 + jnp.tanh(np.sqrt(2.0 / np.pi) * (x + 0.044715 * (x * x * x))))


def _ffn_up_kernel(x_ref, g_ref, wg_ref, wv_ref, cw_ref, cb_ref, o_ref, h_ref, halo_ref):
    i = pl.program_id(0)
    j = pl.program_id(1)

    @pl.when(j == 0)
    def _():
        _store_normed(x_ref, g_ref, h_ref, min(256, x_ref.shape[0]))

    @pl.when(i == 0)
    def _():
        halo_ref[j] = jnp.zeros(halo_ref.shape[1:], F32)

    tm = x_ref.shape[0]
    h = h_ref[...]
    up_g = _dot(h, wg_ref[...].astype(BF16))
    z = _conv3_rows(up_g, halo_ref[j], cw_ref[...]) + cb_ref[...]
    halo_ref[j] = up_g[tm - SUBLANES:, :]
    o_ref[...] = (_gelu_tanh(z) * _dot(h, wv_ref[...].astype(BF16))).astype(o_ref.dtype)


def _ffn_up(x, g, w_up, conv_w, conv_b, layer, tm, tn):
    s, d = x.shape
    f = conv_w.shape[2]
    tm, tn = min(tm, s), min(tn, f)
    nj = f // tn
    return pl.pallas_call(
        _ffn_up_kernel,
        grid=(s // tm, nj),
        in_specs=[pl.BlockSpec((tm, d), lambda i, j: (i, 0)),
                  pl.BlockSpec((None, 1, d), lambda i, j: (layer, 0, 0)),
                  pl.BlockSpec((None, d, tn), lambda i, j: (layer, 0, j)),
                  pl.BlockSpec((None, d, tn), lambda i, j: (layer, 0, nj + j)),
                  pl.BlockSpec((None, CONV_K, tn), lambda i, j: (layer, 0, j)),
                  pl.BlockSpec((None, 1, tn), lambda i, j: (layer, 0, j))],
        out_specs=pl.BlockSpec((tm, tn), lambda i, j: (i, j)),
        out_shape=jax.ShapeDtypeStruct((s, f), BF16),
        scratch_shapes=[pltpu.VMEM((tm, d), BF16), pltpu.VMEM((nj, SUBLANES, tn), F32)],
        compiler_params=_params(("arbitrary", "arbitrary")),
        name="ffn_up",
    )(x, g, w_up, w_up, conv_w, conv_b)


def kernel(x, mem, w_in, conv_mix_w, hg_lower_bounds, hg_norm_w, w_mem_kv, w_branch, w_out,
           norm_mix_pre, norm_mix_post, norm_mem, norm_ffn_pre, norm_ffn_post,
           w_ffn_up, conv_ffn_w, conv_ffn_b, w_ffn_down):
    bsz, s, d = x.shape
    depth = w_in.shape[0]
    conv_width = conv_mix_w.shape[2]
    hg_width = hg_norm_w.shape[1]
    mem_width = w_mem_kv.shape[2] // 2
    hg_col0 = 3 * conv_width
    mq_col0 = hg_col0 + 4 * hg_width
    gate_col0 = mq_col0 + mem_width
    lower_bounds = hg_lower_bounds.astype(F32)

    def rows(p):
        return p.reshape(depth, 1, p.shape[1])

    w_br_b, w_out_b, w_dn_b = (w.astype(BF16) for w in (w_branch, w_out, w_ffn_down))
    g_mix_pre, g_mix_post, g_mem, g_ffn_pre, g_ffn_post, hg_nw, ffn_b = (
        rows(p) for p in (norm_mix_pre, norm_mix_post, norm_mem, norm_ffn_pre, norm_ffn_post,
                          hg_norm_w, conv_ffn_b))

    outs = []
    for b in range(bsz):
        xb = x[b]
        mb = mem[b]
        for l in range(depth):
            proj = _norm_proj(xb, g_mix_pre, w_in, l, 1024, 1024)
            mem_kv = _norm_proj(mb, g_mem, w_mem_kv, l, 256, 1024)
            y_a = _conv_mix(proj, conv_mix_w, l, 512)
            y_b = _hgrn(proj, lower_bounds, hg_nw, l, 1024, hg_col0)
            y_m = _mem_attn(proj, mem_kv, 512, mq_col0)
            merged = _merge(y_a, y_b, y_m, proj, w_br_b, l, 512, 1024, gate_col0)
            xb = _proj_norm_res(merged, w_out_b, xb, g_mix_post, l, 512, d)
            act = _ffn_up(xb, g_ffn_pre, w_ffn_up, conv_ffn_w, ffn_b, l, 1024, 512)
            xb = _proj_norm_res(act, w_dn_b, xb, g_ffn_post, l, 1024, 512)
        outs.append(xb)
    return outs[0][None] if bsz == 1 else jnp.stack(outs)
```

```python
import functools

import numpy as np
import jax
import jax.numpy as jnp
from jax import lax
from jax.experimental import pallas as pl
from jax.experimental.pallas import tpu as pltpu

F32 = jnp.float32
BF16 = jnp.bfloat16

RMS_EPS = 1e-6
LOG_TINY = 1e-30
CONV_K = 3
HG_HEADS = 8
HG_DIM = 128
HG_CHUNK = 64
HG_LEVELS = 6
HG_GROUP = 8
LOG2E = 1.4426950408889634
MEM_HEADS = 4
SUBLANES = 8
VMEM_LIMIT = 56 * 1024 * 1024


def _params(semantics):
    return pltpu.CompilerParams(dimension_semantics=semantics, vmem_limit_bytes=VMEM_LIMIT)


def _dot(a, b):
    return jnp.dot(a, b, preferred_element_type=F32)


def _dot_nt(a, b):
    return lax.dot_general(a, b, (((1,), (1,)), ((), ())), preferred_element_type=F32)


def _dot_tn(a, b):
    return lax.dot_general(a, b, (((0,), (0,)), ((), ())), preferred_element_type=F32)


def _rms_scale(x):
    return lax.rsqrt(jnp.mean(x * x, axis=-1, keepdims=True) + RMS_EPS)


def _sigmoid(x):
    return 1.0 / (1.0 + jnp.exp(-x))


def _store_normed(x_ref, g_ref, h_ref, rows):
    tm = x_ref.shape[0]

    def body(r, _):
        sl = pl.ds(pl.multiple_of(r * rows, rows), rows)
        x = x_ref[sl, :]
        h_ref[sl, :] = (x * _rms_scale(x) * g_ref[...]).astype(BF16)
        return 0

    lax.fori_loop(0, tm // rows, body, 0)


def _norm_proj_kernel(x_ref, g_ref, w_ref, o_ref, h_ref):
    @pl.when(pl.program_id(1) == 0)
    def _():
        _store_normed(x_ref, g_ref, h_ref, min(256, x_ref.shape[0]))

    o_ref[...] = _dot(h_ref[...], w_ref[...].astype(BF16)).astype(o_ref.dtype)


def _norm_proj(x, g, w, layer, tm, tn):
    s, d = x.shape
    n = w.shape[2]
    tm, tn = min(tm, s), min(tn, n)
    return pl.pallas_call(
        _norm_proj_kernel,
        grid=(s // tm, n // tn),
        in_specs=[pl.BlockSpec((tm, d), lambda i, j: (i, 0)),
                  pl.BlockSpec((None, 1, d), lambda i, j: (layer, 0, 0)),
                  pl.BlockSpec((None, d, tn), lambda i, j: (layer, 0, j))],
        out_specs=pl.BlockSpec((tm, tn), lambda i, j: (i, j)),
        out_shape=jax.ShapeDtypeStruct((s, n), BF16),
        scratch_shapes=[pltpu.VMEM((tm, d), BF16)],
        compiler_params=_params(("arbitrary", "arbitrary")),
        name="norm_proj",
    )(x, g, w)


def _conv3_rows(u, halo, w):
    tm = u.shape[0]
    u1 = pltpu.roll(u, 1, 0)
    u2 = pltpu.roll(u, 2, 0)
    y = u2 * w[0:1, :] + u1 * w[1:2, :] + u * w[2:3, :]
    head = jnp.concatenate([halo, u[0:SUBLANES, :]], axis=0)
    h1 = pltpu.roll(head, 1, 0)[SUBLANES:, :]
    h2 = pltpu.roll(head, 2, 0)[SUBLANES:, :]
    y_head = h2 * w[0:1, :] + h1 * w[1:2, :] + u[0:SUBLANES, :] * w[2:3, :]
    if tm == SUBLANES:
        return y_head
    return jnp.concatenate([y_head, y[SUBLANES:, :]], axis=0)


IN_TILE = 1024
IN_SKIP = 2


def _in_proj_kernel(x_ref, g_ref, w_ref, cw_ref, kv_ref, o_ref, h_ref, stash_ref, halo_ref):
    i = pl.program_id(0)
    j = pl.program_id(1)
    tm, tn = o_ref.shape
    half = tn // 2
    hd = tn // MEM_HEADS

    @pl.when(j == 0)
    def _():
        _store_normed(x_ref, g_ref, h_ref, min(256, tm))

    @pl.when(jnp.logical_and(i == 0, j == 0))
    def _():
        halo_ref[...] = jnp.zeros_like(halo_ref)

    def proj(lo, width):
        return _dot(h_ref[...], w_ref[:, lo:lo + width].astype(BF16))

    @pl.when(j < 2)
    def _():
        stash_ref[j] = proj(0, tn).astype(BF16)

    @pl.when(j == 2)
    def _():
        for lo in range(0, tn, hd):
            cols = slice(lo, lo + hd)
            u = stash_ref[1, :, cols].astype(F32) * proj(lo, hd)
            y = _conv3_rows(u, halo_ref[:, cols], cw_ref[:, cols])
            halo_ref[:, cols] = u[tm - SUBLANES:, :]
            o_ref[:, cols] = (stash_ref[0, :, cols].astype(F32) * y).astype(o_ref.dtype)

    @pl.when(jnp.logical_and(j > 2, j != 7))
    def _():
        o_ref[...] = proj(0, tn).astype(o_ref.dtype)

    @pl.when(j == 7)
    def _():
        wb = w_ref[...].astype(BF16)
        slab = min(256, tm)
        heads = range(MEM_HEADS)
        probs = None
        for r in range(tm // slab + 1):
            if r < tm // slab:
                q = _dot(h_ref[r * slab:(r + 1) * slab, :], wb).astype(BF16)
                scores = [_dot_nt(q[:, h * hd:(h + 1) * hd], kv_ref[:, h * hd:(h + 1) * hd])
                          * (hd ** -0.5) for h in heads]
            if probs is not None:
                for h in heads:
                    v = kv_ref[:, tn + h * hd:tn + (h + 1) * hd]
                    o_ref[(r - 1) * slab:r * slab, h * hd:(h + 1) * hd] = (
                        _dot(probs[h], v).astype(o_ref.dtype))
            if r < tm // slab:
                probs = []
                for sc in scores:
                    p = jnp.exp(sc - jnp.max(sc, axis=-1, keepdims=True))
                    probs.append((p / jnp.sum(p, axis=-1, keepdims=True)).astype(BF16))


def _in_proj(x, g, w, conv_w, mem_kv, layer, tm):
    s, d = x.shape
    n = w.shape[2]
    tm, tn = min(tm, s), IN_TILE
    assert conv_w.shape[2] == tn and mem_kv.shape[1] == 2 * tn
    return pl.pallas_call(
        _in_proj_kernel,
        grid=(s // tm, n // tn),
        in_specs=[pl.BlockSpec((tm, d), lambda i, j: (i, 0)),
                  pl.BlockSpec((None, 1, d), lambda i, j: (layer, 0, 0)),
                  pl.BlockSpec((None, d, tn), lambda i, j: (layer, 0, j)),
                  pl.BlockSpec((None, CONV_K, tn), lambda i, j: (layer, 0, 0)),
                  pl.BlockSpec(mem_kv.shape, lambda i, j: (0, 0))],
        out_specs=pl.BlockSpec((tm, tn), lambda i, j: (i, jnp.maximum(j - IN_SKIP, 0))),
        out_shape=jax.ShapeDtypeStruct((s, n - IN_SKIP * tn), BF16),
        scratch_shapes=[pltpu.VMEM((tm, d), BF16), pltpu.VMEM((2, tm, tn), BF16),
                        pltpu.VMEM((SUBLANES, tn), F32)],
        compiler_params=_params(("arbitrary", "arbitrary")),
        name="in_proj",
    )(x, g, w, conv_w, mem_kv)


def _hgrn_pair_levels():
    i = np.arange(HG_CHUNK)[:, None]
    j = np.arange(HG_CHUNK)[None, :]
    lvl = np.full((HG_CHUNK, HG_CHUNK), HG_LEVELS + 1, np.int32)
    lvl[i == j] = 0
    for level in range(1, HG_LEVELS + 1):
        lvl[((i >> level) == (j >> level)) & (((i >> (level - 1)) & 1) == 1)
            & (((j >> (level - 1)) & 1) == 0)] = level
    return lvl


def _level_operand(q, k, b, lvl):
    size = 1 << lvl
    half = size >> 1
    if half >= SUBLANES:
        parts = []
        for lo in range(0, HG_CHUNK, size):
            mid, hi = lo + half, lo + size
            bm = b[mid - 1:mid, :]
            parts.append(k[lo:mid, :] * jnp.exp(bm - b[lo:mid, :]))
            parts.append(q[mid:hi, :] * jnp.exp(b[mid:hi, :] - bm))
        return jnp.concatenate(parts, axis=0).astype(BF16)
    groups = HG_CHUNK // SUBLANES
    shape3 = (groups, SUBLANES, HG_DIM)
    b3, q3, k3 = b.reshape(shape3), q.reshape(shape3), k.reshape(shape3)
    sub = lax.broadcasted_iota(jnp.int32, (1, SUBLANES, HG_DIM), 1)
    upper = ((sub >> (lvl - 1)) & 1) == 1
    if size == 2:
        bm = pltpu.roll(b, 1, 0).reshape(shape3)
        x = jnp.where(upper, (b3 - bm) * LOG2E, 0.0)
    else:
        if size == 8:
            bm = jnp.broadcast_to(b3[:, 3:4, :], shape3)
        else:
            bm = jnp.where(sub < 4, jnp.broadcast_to(b3[:, 1:2, :], shape3),
                           jnp.broadcast_to(b3[:, 5:6, :], shape3))
        x = (b3 - bm) * jnp.where(upper, LOG2E, -LOG2E)
    a = jnp.where(upper, q3, k3) * jnp.exp2(x)
    return a.reshape(HG_CHUNK, HG_DIM).astype(BF16)


def _hgrn_kernel(layer, q_ref, f_ref, i_ref, g_ref, lb_ref, nw_ref, tri_ref, lvl_ref,
                 o_ref, st_ref):
    @pl.when(pl.program_id(1) == 0)
    def _():
        st_ref[...] = jnp.zeros_like(st_ref)

    lbs = lb_ref[...]
    e = jnp.exp(lbs - jnp.max(lbs, axis=0, keepdims=True))
    soft = e / jnp.sum(e, axis=0, keepdims=True)
    lb = jnp.zeros((1, HG_DIM), F32)
    for r in range(1, layer + 1):
        lb = lb + soft[r:r + 1, :]
    lb = jnp.clip(lb, 0.0, 1.0)
    one_minus_lb = 1.0 - lb
    lb_tiny = lb + LOG_TINY
    nw = nw_ref[...]
    tri = tri_ref[...]
    pair_level = lvl_ref[...]
    chunks = range(HG_GROUP)

    def group(gi, _):
        rows = [pl.ds(pl.multiple_of((gi * HG_GROUP + c) * HG_CHUNK, HG_CHUNK), HG_CHUNK)
                for c in chunks]

        q, k, b = [], [], []
        for c in chunks:
            zq = q_ref[rows[c], :].astype(F32)
            zf = f_ref[rows[c], :].astype(F32)
            q.append(zq * _sigmoid(zq) * (HG_DIM ** -0.5))
            sig = _sigmoid(zf)
            logf = jnp.log(lb_tiny + one_minus_lb * sig)
            k.append(one_minus_lb * (1.0 - sig) - LOG_TINY)
            p0 = logf.astype(BF16)
            r0 = logf - p0.astype(F32)
            p1 = r0.astype(BF16)
            p2 = (r0 - p1.astype(F32)).astype(BF16)
            cum3 = _dot(tri, jnp.concatenate([p0, p1, p2], axis=1))
            b.append(cum3[:, :HG_DIM] + cum3[:, HG_DIM:2 * HG_DIM] + cum3[:, 2 * HG_DIM:])

        scores = []
        for c in chunks:
            s = jnp.where(pair_level == 0, _dot_nt(q[c].astype(BF16), k[c].astype(BF16)), 0.0)
            for lvl in range(1, HG_LEVELS + 1):
                a = _level_operand(q[c], k[c], b[c], lvl)
                s = jnp.where(pair_level == lvl, _dot_nt(a, a), s)
            scores.append(s.astype(BF16))

        v = [i_ref[rows[c], :] for c in chunks]
        b_last = [b[c][HG_CHUNK - 1:HG_CHUNK, :] for c in chunks]
        kv = [_dot_tn(v[c], (k[c] * jnp.exp(b_last[c] - b[c])).astype(BF16)) for c in chunks]
        st = st_ref[...]
        o_inter = []
        for c in chunks:
            o_inter.append(_dot_nt((q[c] * jnp.exp(b[c])).astype(BF16), st.astype(BF16)))
            st = st * jnp.exp(b_last[c]) + kv[c]
        st_ref[...] = st

        for c in chunks:
            o = _dot(scores[c], v[c]) + o_inter[c]
            zg = g_ref[rows[c], :].astype(F32)
            o = o * _rms_scale(o) * nw
            o_ref[rows[c], :] = (o * (zg * _sigmoid(zg))).astype(o_ref.dtype)
        return 0

    lax.fori_loop(0, q_ref.shape[0] // (HG_CHUNK * HG_GROUP), group, 0)


def _hgrn(proj, lower_bounds, norm_w, layer, tm, col0):
    s = proj.shape[0]
    tm = min(tm, s)
    assert tm % (HG_CHUNK * HG_GROUP) == 0
    depth = lower_bounds.shape[0]
    width = HG_HEADS * HG_DIM
    cb = col0 // HG_DIM
    tri = jnp.asarray(np.tril(np.ones((HG_CHUNK, HG_CHUNK), np.float32)), BF16)
    pair_levels = jnp.asarray(_hgrn_pair_levels())

    def col(k):
        return pl.BlockSpec((tm, HG_DIM), lambda h, i: (i, cb + k * HG_HEADS + h))

    return pl.pallas_call(
        functools.partial(_hgrn_kernel, layer),
        grid=(HG_HEADS, s // tm),
        in_specs=[col(0), col(1), col(2), col(3),
                  pl.BlockSpec((depth, HG_DIM), lambda h, i: (0, h)),
                  pl.BlockSpec((None, 1, HG_DIM), lambda h, i: (layer, 0, h)),
                  pl.BlockSpec((HG_CHUNK, HG_CHUNK), lambda h, i: (0, 0)),
                  pl.BlockSpec((HG_CHUNK, HG_CHUNK), lambda h, i: (0, 0))],
        out_specs=pl.BlockSpec((tm, HG_DIM), lambda h, i: (i, h)),
        out_shape=jax.ShapeDtypeStruct((s, width), BF16),
        scratch_shapes=[pltpu.VMEM((HG_DIM, HG_DIM), F32)],
        compiler_params=_params(("arbitrary", "arbitrary")),
        name="hgrn2",
    )(proj, proj, proj, proj, lower_bounds, norm_w, tri, pair_levels)


def _merge_kernel(ya_ref, yb_ref, ym_ref, ga_ref, gb_ref, gm_ref, w_ref, o_ref):
    acc = _sigmoid(ga_ref[...].astype(F32)) * _dot(ya_ref[...], w_ref[0])
    acc = acc + _sigmoid(gb_ref[...].astype(F32)) * _dot(yb_ref[...], w_ref[1])
    acc = acc + _sigmoid(gm_ref[...].astype(F32)) * _dot(ym_ref[...], w_ref[2])
    o_ref[...] = acc.astype(o_ref.dtype)


def _merge(proj, y_b, w_branch, layer, tm, tn, ya_col, ym_col, col0):
    s, c = y_b.shape
    d = w_branch.shape[3]
    tm, tn = min(tm, s), min(tn, d)
    gcb = col0 // tn
    nj = d // tn

    def gate(k):
        return pl.BlockSpec((tm, tn), lambda j, i: (i, gcb + k * nj + j))

    def branch(col):
        return pl.BlockSpec((tm, c), lambda j, i: (i, col // c))

    return pl.pallas_call(
        _merge_kernel,
        grid=(nj, s // tm),
        in_specs=[branch(ya_col), branch(0), branch(ym_col), gate(0), gate(1), gate(2),
                  pl.BlockSpec((None, 3, c, tn), lambda j, i: (layer, 0, 0, j))],
        out_specs=pl.BlockSpec((tm, tn), lambda j, i: (i, j)),
        out_shape=jax.ShapeDtypeStruct((s, d), BF16),
        compiler_params=_params(("arbitrary", "arbitrary")),
        name="merge",
    )(proj, y_b, proj, proj, proj, proj, w_branch)


def _proj_norm_res_single_kernel(a_ref, w_ref, x_ref, g_ref, o_ref):
    y = _dot(a_ref[...], w_ref[...])
    o_ref[...] = x_ref[...] + y * _rms_scale(y) * g_ref[...]


def _proj_norm_res_kernel(a_ref, w_ref, x_ref, g_ref, o_ref):
    kk = pl.program_id(1)

    @pl.when(kk == 0)
    def _():
        o_ref[...] = _dot(a_ref[...], w_ref[...])

    @pl.when(kk > 0)
    def _():
        o_ref[...] += _dot(a_ref[...], w_ref[...])

    @pl.when(kk == pl.num_programs(1) - 1)
    def _():
        rows = min(256, o_ref.shape[0])

        def body(r, _):
            sl = pl.ds(pl.multiple_of(r * rows, rows), rows)
            acc = o_ref[sl, :]
            o_ref[sl, :] = x_ref[sl, :] + acc * _rms_scale(acc) * g_ref[...]
            return 0

        lax.fori_loop(0, o_ref.shape[0] // rows, body, 0)


def _proj_norm_res(a, w, x, g, layer, tm, tk):
    s, k = a.shape
    d = w.shape[2]
    tm, tk = min(tm, s), min(tk, k)
    return pl.pallas_call(
        _proj_norm_res_single_kernel if tk == k else _proj_norm_res_kernel,
        grid=(s // tm, k // tk),
        in_specs=[pl.BlockSpec((tm, tk), lambda i, kk: (i, kk)),
                  pl.BlockSpec((None, tk, d), lambda i, kk: (layer, kk, 0)),
                  pl.BlockSpec((tm, d), lambda i, kk: (i, 0)),
                  pl.BlockSpec((None, 1, d), lambda i, kk: (layer, 0, 0))],
        out_specs=pl.BlockSpec((tm, d), lambda i, kk: (i, 0)),
        out_shape=jax.ShapeDtypeStruct((s, d), F32),
        compiler_params=_params(("arbitrary", "arbitrary")),
        name="proj_norm_res",
    )(a, w, x, g)


def _gelu_tanh(x):
    return 0.5 * x * (1.0 + jnp.tanh(np.sqrt(2.0 / np.pi) * (x + 0.044715 * (x * x * x))))


def _ffn_up_kernel(x_ref, g_ref, wg_ref, wv_ref, cw_ref, cb_ref, o_ref, h_ref, halo_ref):
    i = pl.program_id(0)
    j = pl.program_id(1)

    @pl.when(j == 0)
    def _():
        _store_normed(x_ref, g_ref, h_ref, min(256, x_ref.shape[0]))

    @pl.when(i == 0)
    def _():
        halo_ref[j] = jnp.zeros(halo_ref.shape[1:], F32)

    tm = x_ref.shape[0]
    h = h_ref[...]
    up_g = _dot(h, wg_ref[...].astype(BF16))
    z = _conv3_rows(up_g, halo_ref[j], cw_ref[...]) + cb_ref[...]
    halo_ref[j] = up_g[tm - SUBLANES:, :]
    o_ref[...] = (_gelu_tanh(z) * _dot(h, wv_ref[...].astype(BF16))).astype(o_ref.dtype)


def _ffn_up(x, g, w_up, conv_w, conv_b, layer, tm, tn):
    s, d = x.shape
    f = conv_w.shape[2]
    tm, tn = min(tm, s), min(tn, f)
    nj = f // tn
    return pl.pallas_call(
        _ffn_up_kernel,
        grid=(s // tm, nj),
        in_specs=[pl.BlockSpec((tm, d), lambda i, j: (i, 0)),
                  pl.BlockSpec((None, 1, d), lambda i, j: (layer, 0, 0)),
                  pl.BlockSpec((None, d, tn), lambda i, j: (layer, 0, j)),
                  pl.BlockSpec((None, d, tn), lambda i, j: (layer, 0, nj + j)),
                  pl.BlockSpec((None, CONV_K, tn), lambda i, j: (layer, 0, j)),
                  pl.BlockSpec((None, 1, tn), lambda i, j: (layer, 0, j))],
        out_specs=pl.BlockSpec((tm, tn), lambda i, j: (i, j)),
        out_shape=jax.ShapeDtypeStruct((s, f), BF16),
        scratch_shapes=[pltpu.VMEM((tm, d), BF16), pltpu.VMEM((nj, SUBLANES, tn), F32)],
        compiler_params=_params(("arbitrary", "arbitrary")),
        name="ffn_up",
    )(x, g, w_up, w_up, conv_w, conv_b)


def kernel(x, mem, w_in, conv_mix_w, hg_lower_bounds, hg_norm_w, w_mem_kv, w_branch, w_out,
           norm_mix_pre, norm_mix_post, norm_mem, norm_ffn_pre, norm_ffn_post,
           w_ffn_up, conv_ffn_w, conv_ffn_b, w_ffn_down):
    bsz, s, d = x.shape
    depth = w_in.shape[0]
    conv_width = conv_mix_w.shape[2]
    hg_width = hg_norm_w.shape[1]
    mem_width = w_mem_kv.shape[2] // 2
    assert conv_width == hg_width == mem_width == IN_TILE
    ya_col = 0
    hg_col0 = IN_TILE
    ym_col = hg_col0 + 4 * hg_width
    gate_col0 = ym_col + mem_width
    lower_bounds = hg_lower_bounds.astype(F32)

    def rows(p):
        return p.reshape(depth, 1, p.shape[1])

    w_br_b, w_out_b, w_dn_b = (w.astype(BF16) for w in (w_branch, w_out, w_ffn_down))
    g_mix_pre, g_mix_post, g_mem, g_ffn_pre, g_ffn_post, hg_nw, ffn_b = (
        rows(p) for p in (norm_mix_pre, norm_mix_post, norm_mem, norm_ffn_pre, norm_ffn_post,
                          hg_norm_w, conv_ffn_b))

    outs = []
    for b in range(bsz):
        xb = x[b]
        mb = mem[b]
        for l in range(depth):
            mem_kv = _norm_proj(mb, g_mem, w_mem_kv, l, 256, 1024)
            proj = _in_proj(xb, g_mix_pre, w_in, conv_mix_w, mem_kv, l, 1024)
            y_b = _hgrn(proj, lower_bounds, hg_nw, l, 1024, hg_col0)
            merged = _merge(proj, y_b, w_br_b, l, 512, 1024, ya_col, ym_col, gate_col0)
            xb = _proj_norm_res(merged, w_out_b, xb, g_mix_post, l, 512, d)
            act = _ffn_up(xb, g_ffn_pre, w_ffn_up, conv_ffn_w, ffn_b, l, 1024, 512)
            xb = _proj_norm_res(act, w_dn_b, xb, g_ffn_post, l, 1024, 512)
        outs.append(xb)
    return outs[0][None] if bsz == 1 else jnp.stack(outs)
```

```python
import functools

import numpy as np
import jax
import jax.numpy as jnp
from jax import lax
from jax.experimental import pallas as pl
from jax.experimental.pallas import tpu as pltpu

F32 = jnp.float32
BF16 = jnp.bfloat16

RMS_EPS = 1e-6
LOG_TINY = 1e-30
CONV_K = 3
HG_HEADS = 8
HG_DIM = 128
HG_CHUNK = 64
HG_LEVELS = 6
HG_GROUP = 16
LOG2E = 1.4426950408889634
MEM_HEADS = 4
SUBLANES = 8
VMEM_LIMIT = 56 * 1024 * 1024


def _params(semantics):
    return pltpu.CompilerParams(dimension_semantics=semantics, vmem_limit_bytes=VMEM_LIMIT)


def _dot(a, b):
    return jnp.dot(a, b, preferred_element_type=F32)


def _dot_nt(a, b):
    return lax.dot_general(a, b, (((1,), (1,)), ((), ())), preferred_element_type=F32)


def _dot_tn(a, b):
    return lax.dot_general(a, b, (((0,), (0,)), ((), ())), preferred_element_type=F32)


def _rms_scale(x):
    return lax.rsqrt(jnp.mean(x * x, axis=-1, keepdims=True) + RMS_EPS)


def _sigmoid(x):
    return 1.0 / (1.0 + jnp.exp2(x * -LOG2E))


def _store_normed(x_ref, g_ref, h_ref, rows):
    tm = x_ref.shape[0]

    def body(r, _):
        sl = pl.ds(pl.multiple_of(r * rows, rows), rows)
        x = x_ref[sl, :]
        h_ref[sl, :] = (x * _rms_scale(x) * g_ref[...]).astype(BF16)
        return 0

    lax.fori_loop(0, tm // rows, body, 0)


def _norm_proj_kernel(x_ref, g_ref, w_ref, o_ref, h_ref):
    @pl.when(pl.program_id(1) == 0)
    def _():
        _store_normed(x_ref, g_ref, h_ref, min(256, x_ref.shape[0]))

    o_ref[...] = _dot(h_ref[...], w_ref[...].astype(BF16)).astype(o_ref.dtype)


def _norm_proj(x, g, w, layer, tm, tn):
    s, d = x.shape
    n = w.shape[2]
    tm, tn = min(tm, s), min(tn, n)
    return pl.pallas_call(
        _norm_proj_kernel,
        grid=(s // tm, n // tn),
        in_specs=[pl.BlockSpec((tm, d), lambda i, j: (i, 0)),
                  pl.BlockSpec((None, 1, d), lambda i, j: (layer, 0, 0)),
                  pl.BlockSpec((None, d, tn), lambda i, j: (layer, 0, j))],
        out_specs=pl.BlockSpec((tm, tn), lambda i, j: (i, j)),
        out_shape=jax.ShapeDtypeStruct((s, n), BF16),
        scratch_shapes=[pltpu.VMEM((tm, d), BF16)],
        compiler_params=_params(("arbitrary", "arbitrary")),
        name="norm_proj",
    )(x, g, w)


def _conv3_rows(u, halo, w):
    tm = u.shape[0]
    u1 = pltpu.roll(u, 1, 0)
    u2 = pltpu.roll(u, 2, 0)
    y = u2 * w[0:1, :] + u1 * w[1:2, :] + u * w[2:3, :]
    head = jnp.concatenate([halo, u[0:SUBLANES, :]], axis=0)
    h1 = pltpu.roll(head, 1, 0)[SUBLANES:, :]
    h2 = pltpu.roll(head, 2, 0)[SUBLANES:, :]
    y_head = h2 * w[0:1, :] + h1 * w[1:2, :] + u[0:SUBLANES, :] * w[2:3, :]
    if tm == SUBLANES:
        return y_head
    return jnp.concatenate([y_head, y[SUBLANES:, :]], axis=0)


IN_TILE = 1024
IN_SKIP = 2


def _in_proj_kernel(x_ref, g_ref, w_ref, cw_ref, kv_ref, o_ref, h_ref, stash_ref, halo_ref):
    i = pl.program_id(0)
    j = pl.program_id(1)
    tm, tn = o_ref.shape
    half = tn // 2
    hd = tn // MEM_HEADS

    @pl.when(j == 0)
    def _():
        _store_normed(x_ref, g_ref, h_ref, min(256, tm))

    @pl.when(jnp.logical_and(i == 0, j == 0))
    def _():
        halo_ref[...] = jnp.zeros_like(halo_ref)

    def proj(lo, width):
        return _dot(h_ref[...], w_ref[:, lo:lo + width].astype(BF16))

    @pl.when(j < 2)
    def _():
        stash_ref[j] = proj(0, tn).astype(BF16)

    @pl.when(j == 2)
    def _():
        for lo in range(0, tn, hd):
            cols = slice(lo, lo + hd)
            u = stash_ref[1, :, cols].astype(F32) * proj(lo, hd)
            y = _conv3_rows(u, halo_ref[:, cols], cw_ref[:, cols])
            halo_ref[:, cols] = u[tm - SUBLANES:, :]
            o_ref[:, cols] = (stash_ref[0, :, cols].astype(F32) * y).astype(o_ref.dtype)

    @pl.when(jnp.logical_and(j > 2, j != 7))
    def _():
        o_ref[...] = proj(0, tn).astype(o_ref.dtype)

    @pl.when(j == 7)
    def _():
        wb = w_ref[...].astype(BF16)
        slab = min(256, tm)
        heads = range(MEM_HEADS)
        probs = None
        for r in range(tm // slab + 1):
            if r < tm // slab:
                q = _dot(h_ref[r * slab:(r + 1) * slab, :], wb).astype(BF16)
                scores = [_dot_nt(q[:, h * hd:(h + 1) * hd], kv_ref[:, h * hd:(h + 1) * hd])
                          * (hd ** -0.5) for h in heads]
            if probs is not None:
                for h in heads:
                    v = kv_ref[:, tn + h * hd:tn + (h + 1) * hd]
                    o_ref[(r - 1) * slab:r * slab, h * hd:(h + 1) * hd] = (
                        _dot(probs[h], v).astype(o_ref.dtype))
            if r < tm // slab:
                probs = []
                for sc in scores:
                    p = jnp.exp(sc - jnp.max(sc, axis=-1, keepdims=True))
                    probs.append((p / jnp.sum(p, axis=-1, keepdims=True)).astype(BF16))


def _in_proj(x, g, w, conv_w, mem_kv, layer, tm):
    s, d = x.shape
    n = w.shape[2]
    tm, tn = min(tm, s), IN_TILE
    assert conv_w.shape[2] == tn and mem_kv.shape[1] == 2 * tn
    return pl.pallas_call(
        _in_proj_kernel,
        grid=(s // tm, n // tn),
        in_specs=[pl.BlockSpec((tm, d), lambda i, j: (i, 0)),
                  pl.BlockSpec((None, 1, d), lambda i, j: (layer, 0, 0)),
                  pl.BlockSpec((None, d, tn), lambda i, j: (layer, 0, j)),
                  pl.BlockSpec((None, CONV_K, tn), lambda i, j: (layer, 0, 0)),
                  pl.BlockSpec(mem_kv.shape, lambda i, j: (0, 0))],
        out_specs=pl.BlockSpec((tm, tn), lambda i, j: (i, jnp.maximum(j - IN_SKIP, 0))),
        out_shape=jax.ShapeDtypeStruct((s, n - IN_SKIP * tn), BF16),
        scratch_shapes=[pltpu.VMEM((tm, d), BF16), pltpu.VMEM((2, tm, tn), BF16),
                        pltpu.VMEM((SUBLANES, tn), F32)],
        compiler_params=_params(("arbitrary", "arbitrary")),
        name="in_proj",
    )(x, g, w, conv_w, mem_kv)


def _hgrn_pair_levels():
    i = np.arange(HG_CHUNK)[:, None]
    j = np.arange(HG_CHUNK)[None, :]
    lvl = np.full((HG_CHUNK, HG_CHUNK), HG_LEVELS + 1, np.int32)
    lvl[i == j] = 0
    for level in range(1, HG_LEVELS + 1):
        lvl[((i >> level) == (j >> level)) & (((i >> (level - 1)) & 1) == 1)
            & (((j >> (level - 1)) & 1) == 0)] = level
    return lvl


def _level_operand(q, k, b, lvl):
    size = 1 << lvl
    half = size >> 1
    if half >= SUBLANES:
        parts = []
        for lo in range(0, HG_CHUNK, size):
            mid, hi = lo + half, lo + size
            bm = b[mid - 1:mid, :]
            parts.append(k[lo:mid, :] * jnp.exp(bm - b[lo:mid, :]))
            parts.append(q[mid:hi, :] * jnp.exp(b[mid:hi, :] - bm))
        return jnp.concatenate(parts, axis=0).astype(BF16)
    groups = HG_CHUNK // SUBLANES
    shape3 = (groups, SUBLANES, HG_DIM)
    b3, q3, k3 = b.reshape(shape3), q.reshape(shape3), k.reshape(shape3)
    sub = lax.broadcasted_iota(jnp.int32, (1, SUBLANES, HG_DIM), 1)
    upper = ((sub >> (lvl - 1)) & 1) == 1
    if size == 2:
        bm = pltpu.roll(b, 1, 0).reshape(shape3)
        x = jnp.where(upper, (b3 - bm) * LOG2E, 0.0)
    else:
        if size == 8:
            bm = jnp.broadcast_to(b3[:, 3:4, :], shape3)
        else:
            bm = jnp.where(sub < 4, jnp.broadcast_to(b3[:, 1:2, :], shape3),
                           jnp.broadcast_to(b3[:, 5:6, :], shape3))
        x = (b3 - bm) * jnp.where(upper, LOG2E, -LOG2E)
    a = jnp.where(upper, q3, k3) * jnp.exp2(x)
    return a.reshape(HG_CHUNK, HG_DIM).astype(BF16)


def _hgrn_kernel(layer, q_ref, f_ref, i_ref, g_ref, lb_ref, nw_ref, tri_ref, lvl_ref,
                 o_ref, st_ref):
    @pl.when(pl.program_id(1) == 0)
    def _():
        st_ref[...] = jnp.zeros_like(st_ref)

    lbs = lb_ref[...]
    e = jnp.exp(lbs - jnp.max(lbs, axis=0, keepdims=True))
    soft = e / jnp.sum(e, axis=0, keepdims=True)
    lb = jnp.zeros((1, HG_DIM), F32)
    for r in range(1, layer + 1):
        lb = lb + soft[r:r + 1, :]
    lb = jnp.clip(lb, 0.0, 1.0)
    one_minus_lb = 1.0 - lb
    lb_tiny = lb + LOG_TINY
    nw = nw_ref[...]
    tri = tri_ref[...]
    pair_level = lvl_ref[...]
    chunks = range(HG_GROUP)

    def group(gi, _):
        rows = [pl.ds(pl.multiple_of((gi * HG_GROUP + c) * HG_CHUNK, HG_CHUNK), HG_CHUNK)
                for c in chunks]

        q, k, b = [], [], []
        for c in chunks:
            zq = q_ref[rows[c], :].astype(F32)
            zf = f_ref[rows[c], :].astype(F32)
            q.append(zq * _sigmoid(zq) * (HG_DIM ** -0.5))
            sig = _sigmoid(zf)
            logf = jnp.log(lb_tiny + one_minus_lb * sig)
            k.append(one_minus_lb * (1.0 - sig) - LOG_TINY)
            p0 = logf.astype(BF16)
            r0 = logf - p0.astype(F32)
            p1 = r0.astype(BF16)
            p2 = (r0 - p1.astype(F32)).astype(BF16)
            cum3 = _dot(tri, jnp.concatenate([p0, p1, p2], axis=1))
            b.append(cum3[:, :HG_DIM] + cum3[:, HG_DIM:2 * HG_DIM] + cum3[:, 2 * HG_DIM:])

        scores = []
        for c in chunks:
            s = jnp.where(pair_level == 0, _dot_nt(q[c].astype(BF16), k[c].astype(BF16)), 0.0)
            for lvl in range(1, HG_LEVELS + 1):
                a = _level_operand(q[c], k[c], b[c], lvl)
                half = 1 << (lvl - 1)
                if half < SUBLANES:
                    s = jnp.where(pair_level == lvl, _dot_nt(a, a), s)
                    continue
                ups = [(lo + half, lo + 2 * half) for lo in range(0, HG_CHUNK, 2 * half)]
                s_up = _dot_nt(jnp.concatenate([a[m:h, :] for m, h in ups], axis=0), a)
                parts, at, row = [], 0, 0
                for m, h in ups:
                    parts.append(s[row:m, :])
                    parts.append(jnp.where(pair_level[m:h, :] == lvl, s_up[at:at + half, :], s[m:h, :]))
                    at, row = at + half, h
                s = jnp.concatenate(parts, axis=0)
            scores.append(s.astype(BF16))

        v = [i_ref[rows[c], :] for c in chunks]
        b_last = [b[c][HG_CHUNK - 1:HG_CHUNK, :] for c in chunks]
        kv = [_dot_tn(v[c], (k[c] * jnp.exp(b_last[c] - b[c])).astype(BF16)) for c in chunks]
        st = st_ref[...]
        o_inter = []
        for c in chunks:
            o_inter.append(_dot_nt((q[c] * jnp.exp(b[c])).astype(BF16), st.astype(BF16)))
            st = st * jnp.exp(b_last[c]) + kv[c]
        st_ref[...] = st

        for c in chunks:
            o = _dot(scores[c], v[c]) + o_inter[c]
            zg = g_ref[rows[c], :].astype(F32)
            o = o * _rms_scale(o) * nw
            o_ref[rows[c], :] = (o * (zg * _sigmoid(zg))).astype(o_ref.dtype)
        return 0

    lax.fori_loop(0, q_ref.shape[0] // (HG_CHUNK * HG_GROUP), group, 0)


def _hgrn(proj, lower_bounds, norm_w, layer, tm, col0):
    s = proj.shape[0]
    tm = min(tm, s)
    assert tm % (HG_CHUNK * HG_GROUP) == 0
    depth = lower_bounds.shape[0]
    width = HG_HEADS * HG_DIM
    cb = col0 // HG_DIM
    tri = jnp.asarray(np.tril(np.ones((HG_CHUNK, HG_CHUNK), np.float32)), BF16)
    pair_levels = jnp.asarray(_hgrn_pair_levels())

    def col(k):
        return pl.BlockSpec((tm, HG_DIM), lambda h, i: (i, cb + k * HG_HEADS + h))

    return pl.pallas_call(
        functools.partial(_hgrn_kernel, layer),
        grid=(HG_HEADS, s // tm),
        in_specs=[col(0), col(1), col(2), col(3),
                  pl.BlockSpec((depth, HG_DIM), lambda h, i: (0, h)),
                  pl.BlockSpec((None, 1, HG_DIM), lambda h, i: (layer, 0, h)),
                  pl.BlockSpec((HG_CHUNK, HG_CHUNK), lambda h, i: (0, 0)),
                  pl.BlockSpec((HG_CHUNK, HG_CHUNK), lambda h, i: (0, 0))],
        out_specs=pl.BlockSpec((tm, HG_DIM), lambda h, i: (i, h)),
        out_shape=jax.ShapeDtypeStruct((s, width), BF16),
        scratch_shapes=[pltpu.VMEM((HG_DIM, HG_DIM), F32)],
        compiler_params=_params(("arbitrary", "arbitrary")),
        name="hgrn2",
    )(proj, proj, proj, proj, lower_bounds, norm_w, tri, pair_levels)


def _merge_kernel(ya_ref, yb_ref, ym_ref, ga_ref, gb_ref, gm_ref, w_ref, o_ref):
    acc = _sigmoid(ga_ref[...].astype(F32)) * _dot(ya_ref[...], w_ref[0])
    acc = acc + _sigmoid(gb_ref[...].astype(F32)) * _dot(yb_ref[...], w_ref[1])
    acc = acc + _sigmoid(gm_ref[...].astype(F32)) * _dot(ym_ref[...], w_ref[2])
    o_ref[...] = acc.astype(o_ref.dtype)


def _merge(proj, y_b, w_branch, layer, tm, tn, ya_col, ym_col, col0):
    s, c = y_b.shape
    d = w_branch.shape[3]
    tm, tn = min(tm, s), min(tn, d)
    gcb = col0 // tn
    nj = d // tn

    def gate(k):
        return pl.BlockSpec((tm, tn), lambda j, i: (i, gcb + k * nj + j))

    def branch(col):
        return pl.BlockSpec((tm, c), lambda j, i: (i, col // c))

    return pl.pallas_call(
        _merge_kernel,
        grid=(nj, s // tm),
        in_specs=[branch(ya_col), branch(0), branch(ym_col), gate(0), gate(1), gate(2),
                  pl.BlockSpec((None, 3, c, tn), lambda j, i: (layer, 0, 0, j))],
        out_specs=pl.BlockSpec((tm, tn), lambda j, i: (i, j)),
        out_shape=jax.ShapeDtypeStruct((s, d), BF16),
        compiler_params=_params(("arbitrary", "arbitrary")),
        name="merge",
    )(proj, y_b, proj, proj, proj, proj, w_branch)


def _proj_norm_res_single_kernel(a_ref, w_ref, x_ref, g_ref, o_ref):
    y = _dot(a_ref[...], w_ref[...])
    o_ref[...] = x_ref[...] + y * _rms_scale(y) * g_ref[...]


def _proj_norm_res_kernel(a_ref, w_ref, x_ref, g_ref, o_ref):
    kk = pl.program_id(1)

    @pl.when(kk == 0)
    def _():
        o_ref[...] = _dot(a_ref[...], w_ref[...])

    @pl.when(kk > 0)
    def _():
        o_ref[...] += _dot(a_ref[...], w_ref[...])

    @pl.when(kk == pl.num_programs(1) - 1)
    def _():
        rows = min(256, o_ref.shape[0])

        def body(r, _):
            sl = pl.ds(pl.multiple_of(r * rows, rows), rows)
            acc = o_ref[sl, :]
            o_ref[sl, :] = x_ref[sl, :] + acc * _rms_scale(acc) * g_ref[...]
            return 0

        lax.fori_loop(0, o_ref.shape[0] // rows, body, 0)


def _proj_norm_res(a, w, x, g, layer, tm, tk):
    s, k = a.shape
    d = w.shape[2]
    tm, tk = min(tm, s), min(tk, k)
    return pl.pallas_call(
        _proj_norm_res_single_kernel if tk == k else _proj_norm_res_kernel,
        grid=(s // tm, k // tk),
        in_specs=[pl.BlockSpec((tm, tk), lambda i, kk: (i, kk)),
                  pl.BlockSpec((None, tk, d), lambda i, kk: (layer, kk, 0)),
                  pl.BlockSpec((tm, d), lambda i, kk: (i, 0)),
                  pl.BlockSpec((None, 1, d), lambda i, kk: (layer, 0, 0))],
        out_specs=pl.BlockSpec((tm, d), lambda i, kk: (i, 0)),
        out_shape=jax.ShapeDtypeStruct((s, d), F32),
        compiler_params=_params(("arbitrary", "arbitrary")),
        name="proj_norm_res",
    )(a, w, x, g)


def _gelu_tanh(x):
    return 0.5 * x * (1.0 + jnp.tanh(np.sqrt(2.0 / np.pi) * (x + 0.044715 * (x * x * x))))


def _ffn_up_kernel(x_ref, g_ref, wg_ref, wv_ref, cw_ref, cb_ref, o_ref, h_ref, halo_ref):
    i = pl.program_id(0)
    j = pl.program_id(1)

    @pl.when(j == 0)
    def _():
        _store_normed(x_ref, g_ref, h_ref, min(256, x_ref.shape[0]))

    @pl.when(i == 0)
    def _():
        halo_ref[j] = jnp.zeros(halo_ref.shape[1:], F32)

    tm = x_ref.shape[0]
    h = h_ref[...]
    up_g = _dot(h, wg_ref[...].astype(BF16))
    z = _conv3_rows(up_g, halo_ref[j], cw_ref[...]) + cb_ref[...]
    halo_ref[j] = up_g[tm - SUBLANES:, :]
    o_ref[...] = (_gelu_tanh(z) * _dot(h, wv_ref[...].astype(BF16))).astype(o_ref.dtype)


def _ffn_up(x, g, w_up, conv_w, conv_b, layer, tm, tn):
    s, d = x.shape
    f = conv_w.shape[2]
    tm, tn = min(tm, s), min(tn, f)
    nj = f // tn
    return pl.pallas_call(
        _ffn_up_kernel,
        grid=(s // tm, nj),
        in_specs=[pl.BlockSpec((tm, d), lambda i, j: (i, 0)),
                  pl.BlockSpec((None, 1, d), lambda i, j: (layer, 0, 0)),
                  pl.BlockSpec((None, d, tn), lambda i, j: (layer, 0, j)),
                  pl.BlockSpec((None, d, tn), lambda i, j: (layer, 0, nj + j)),
                  pl.BlockSpec((None, CONV_K, tn), lambda i, j: (layer, 0, j)),
                  pl.BlockSpec((None, 1, tn), lambda i, j: (layer, 0, j))],
        out_specs=pl.BlockSpec((tm, tn), lambda i, j: (i, j)),
        out_shape=jax.ShapeDtypeStruct((s, f), BF16),
        scratch_shapes=[pltpu.VMEM((tm, d), BF16), pltpu.VMEM((nj, SUBLANES, tn), F32)],
        compiler_params=_params(("arbitrary", "arbitrary")),
        name="ffn_up",
    )(x, g, w_up, w_up, conv_w, conv_b)


def kernel(x, mem, w_in, conv_mix_w, hg_lower_bounds, hg_norm_w, w_mem_kv, w_branch, w_out,
           norm_mix_pre, norm_mix_post, norm_mem, norm_ffn_pre, norm_ffn_post,
           w_ffn_up, conv_ffn_w, conv_ffn_b, w_ffn_down):
    bsz, s, d = x.shape
    depth = w_in.shape[0]
    conv_width = conv_mix_w.shape[2]
    hg_width = hg_norm_w.shape[1]
    mem_width = w_mem_kv.shape[2] // 2
    assert conv_width == hg_width == mem_width == IN_TILE
    ya_col = 0
    hg_col0 = IN_TILE
    ym_col = hg_col0 + 4 * hg_width
    gate_col0 = ym_col + mem_width
    lower_bounds = hg_lower_bounds.astype(F32)

    def rows(p):
        return p.reshape(depth, 1, p.shape[1])

    w_br_b, w_out_b, w_dn_b = (w.astype(BF16) for w in (w_branch, w_out, w_ffn_down))
    g_mix_pre, g_mix_post, g_mem, g_ffn_pre, g_ffn_post, hg_nw, ffn_b = (
        rows(p) for p in (norm_mix_pre, norm_mix_post, norm_mem, norm_ffn_pre, norm_ffn_post,
                          hg_norm_w, conv_ffn_b))

    outs = []
    for b in range(bsz):
        xb = x[b]
        mb = mem[b]
        for l in range(depth):
            mem_kv = _norm_proj(mb, g_mem, w_mem_kv, l, 256, 1024)
            proj = _in_proj(xb, g_mix_pre, w_in, conv_mix_w, mem_kv, l, 1024)
            y_b = _hgrn(proj, lower_bounds, hg_nw, l, 1024, hg_col0)
            merged = _merge(proj, y_b, w_br_b, l, 512, 1024, ya_col, ym_col, gate_col0)
            xb = _proj_norm_res(merged, w_out_b, xb, g_mix_post, l, 512, d)
            act = _ffn_up(xb, g_ffn_pre, w_ffn_up, conv_ffn_w, ffn_b, l, 1024, 512)
            xb = _proj_norm_res(act, w_dn_b, xb, g_ffn_post, l, 1024, 1408)
        outs.append(xb)
    return outs[0][None] if bsz == 1 else jnp.stack(outs)
```

```python
import functools

import numpy as np
import jax
import jax.numpy as jnp
from jax import lax
from jax.experimental import pallas as pl
from jax.experimental.pallas import tpu as pltpu

F32 = jnp.float32
BF16 = jnp.bfloat16

RMS_EPS = 1e-6
LOG_TINY = 1e-30
CONV_K = 3
HG_HEADS = 8
HG_DIM = 128
HG_CHUNK = 64
HG_LEVELS = 6
LOG2E = 1.4426950408889634
MEM_HEADS = 4
SUBLANES = 8
VMEM_LIMIT = 56 * 1024 * 1024


def _params(semantics):
    return pltpu.CompilerParams(dimension_semantics=semantics, vmem_limit_bytes=VMEM_LIMIT)


def _dot(a, b):
    return jnp.dot(a, b, preferred_element_type=F32)


def _dot_nt(a, b):
    return lax.dot_general(a, b, (((1,), (1,)), ((), ())), preferred_element_type=F32)


def _dot_tn(a, b):
    return lax.dot_general(a, b, (((0,), (0,)), ((), ())), preferred_element_type=F32)


def _rms_scale(x):
    return lax.rsqrt(jnp.mean(x * x, axis=-1, keepdims=True) + RMS_EPS)


def _sigmoid(x):
    return 1.0 / (1.0 + jnp.exp2(x * -LOG2E))


def _store_normed(x_ref, g_ref, h_ref, rows):
    tm = x_ref.shape[0]

    def body(r, _):
        sl = pl.ds(pl.multiple_of(r * rows, rows), rows)
        x = x_ref[sl, :]
        h_ref[sl, :] = (x * _rms_scale(x) * g_ref[...]).astype(BF16)
        return 0

    lax.fori_loop(0, tm // rows, body, 0)


def _norm_proj_kernel(x_ref, g_ref, w_ref, o_ref, h_ref):
    @pl.when(pl.program_id(1) == 0)
    def _():
        _store_normed(x_ref, g_ref, h_ref, min(256, x_ref.shape[0]))

    o_ref[...] = _dot(h_ref[...], w_ref[...].astype(BF16)).astype(o_ref.dtype)


def _norm_proj(x, g, w, layer, tm, tn):
    s, d = x.shape
    n = w.shape[2]
    tm, tn = min(tm, s), min(tn, n)
    return pl.pallas_call(
        _norm_proj_kernel,
        grid=(s // tm, n // tn),
        in_specs=[pl.BlockSpec((tm, d), lambda i, j: (i, 0)),
                  pl.BlockSpec((None, 1, d), lambda i, j: (layer, 0, 0)),
                  pl.BlockSpec((None, d, tn), lambda i, j: (layer, 0, j))],
        out_specs=pl.BlockSpec((tm, tn), lambda i, j: (i, j)),
        out_shape=jax.ShapeDtypeStruct((s, n), BF16),
        scratch_shapes=[pltpu.VMEM((tm, d), BF16)],
        compiler_params=_params(("arbitrary", "arbitrary")),
        name="norm_proj",
    )(x, g, w)


def _conv3_rows(u, halo, w):
    tm = u.shape[0]
    u1 = pltpu.roll(u, 1, 0)
    u2 = pltpu.roll(u, 2, 0)
    y = u2 * w[0:1, :] + u1 * w[1:2, :] + u * w[2:3, :]
    head = jnp.concatenate([halo, u[0:SUBLANES, :]], axis=0)
    h1 = pltpu.roll(head, 1, 0)[SUBLANES:, :]
    h2 = pltpu.roll(head, 2, 0)[SUBLANES:, :]
    y_head = h2 * w[0:1, :] + h1 * w[1:2, :] + u[0:SUBLANES, :] * w[2:3, :]
    if tm == SUBLANES:
        return y_head
    return jnp.concatenate([y_head, y[SUBLANES:, :]], axis=0)


IN_TILE = 1024
IN_SKIP = 2


def _in_proj_kernel(x_ref, g_ref, w_ref, cw_ref, kv_ref, o_ref, h_ref, stash_ref, halo_ref):
    i = pl.program_id(0)
    j = pl.program_id(1)
    tm, tn = o_ref.shape
    hd = tn // MEM_HEADS

    @pl.when(j == 0)
    def _():
        _store_normed(x_ref, g_ref, h_ref, min(256, tm))

    @pl.when(jnp.logical_and(i == 0, j == 0))
    def _():
        halo_ref[...] = jnp.zeros_like(halo_ref)

    def proj(lo, width):
        return _dot(h_ref[...], w_ref[:, lo:lo + width].astype(BF16))

    @pl.when(j < 2)
    def _():
        stash_ref[j] = proj(0, tn).astype(BF16)

    @pl.when(j == 2)
    def _():
        for lo in range(0, tn, hd):
            cols = slice(lo, lo + hd)
            u = stash_ref[1, :, cols].astype(F32) * proj(lo, hd)
            y = _conv3_rows(u, halo_ref[:, cols], cw_ref[:, cols])
            halo_ref[:, cols] = u[tm - SUBLANES:, :]
            o_ref[:, cols] = (stash_ref[0, :, cols].astype(F32) * y).astype(o_ref.dtype)

    @pl.when(jnp.logical_and(j > 2, j != 7))
    def _():
        o_ref[...] = proj(0, tn).astype(o_ref.dtype)

    @pl.when(j == 7)
    def _():
        wb = w_ref[...].astype(BF16)
        slab = min(256, tm)
        heads = range(MEM_HEADS)
        probs = None
        for r in range(tm // slab + 1):
            if r < tm // slab:
                q = _dot(h_ref[r * slab:(r + 1) * slab, :], wb).astype(BF16)
                scores = [_dot_nt(q[:, h * hd:(h + 1) * hd], kv_ref[:, h * hd:(h + 1) * hd])
                          * (hd ** -0.5) for h in heads]
            if probs is not None:
                for h in heads:
                    v = kv_ref[:, tn + h * hd:tn + (h + 1) * hd]
                    o_ref[(r - 1) * slab:r * slab, h * hd:(h + 1) * hd] = (
                        _dot(probs[h], v).astype(o_ref.dtype))
            if r < tm // slab:
                probs = []
                for sc in scores:
                    p = jnp.exp(sc - jnp.max(sc, axis=-1, keepdims=True))
                    probs.append((p / jnp.sum(p, axis=-1, keepdims=True)).astype(BF16))


def _in_proj(x, g, w, conv_w, mem_kv, layer, tm):
    s, d = x.shape
    n = w.shape[2]
    tm, tn = min(tm, s), IN_TILE
    assert conv_w.shape[2] == tn and mem_kv.shape[1] == 2 * tn
    return pl.pallas_call(
        _in_proj_kernel,
        grid=(s // tm, n // tn),
        in_specs=[pl.BlockSpec((tm, d), lambda i, j: (i, 0)),
                  pl.BlockSpec((None, 1, d), lambda i, j: (layer, 0, 0)),
                  pl.BlockSpec((None, d, tn), lambda i, j: (layer, 0, j)),
                  pl.BlockSpec((None, CONV_K, tn), lambda i, j: (layer, 0, 0)),
                  pl.BlockSpec(mem_kv.shape, lambda i, j: (0, 0))],
        out_specs=pl.BlockSpec((tm, tn), lambda i, j: (i, jnp.maximum(j - IN_SKIP, 0))),
        out_shape=jax.ShapeDtypeStruct((s, n - IN_SKIP * tn), BF16),
        scratch_shapes=[pltpu.VMEM((tm, d), BF16), pltpu.VMEM((2, tm, tn), BF16),
                        pltpu.VMEM((SUBLANES, tn), F32)],
        compiler_params=_params(("arbitrary", "arbitrary")),
        name="in_proj",
    )(x, g, w, conv_w, mem_kv)


def _hgrn_pair_levels():
    i = np.arange(HG_CHUNK)[:, None]
    j = np.arange(HG_CHUNK)[None, :]
    lvl = np.full((HG_CHUNK, HG_CHUNK), HG_LEVELS + 1, np.int32)
    lvl[i == j] = 0
    for level in range(1, HG_LEVELS + 1):
        lvl[((i >> level) == (j >> level)) & (((i >> (level - 1)) & 1) == 1)
            & (((j >> (level - 1)) & 1) == 0)] = level
    return lvl


def _level_operand(q, k, b, lvl):
    size = 1 << lvl
    half = size >> 1
    if half >= SUBLANES:
        parts = []
        for lo in range(0, HG_CHUNK, size):
            mid, hi = lo + half, lo + size
            bm = b[mid - 1:mid, :]
            parts.append(k[lo:mid, :] * jnp.exp(bm - b[lo:mid, :]))
            parts.append(q[mid:hi, :] * jnp.exp(b[mid:hi, :] - bm))
        return jnp.concatenate(parts, axis=0).astype(BF16)
    groups = HG_CHUNK // SUBLANES
    shape3 = (groups, SUBLANES, HG_DIM)
    b3, q3, k3 = b.reshape(shape3), q.reshape(shape3), k.reshape(shape3)
    sub = lax.broadcasted_iota(jnp.int32, (1, SUBLANES, HG_DIM), 1)
    upper = ((sub >> (lvl - 1)) & 1) == 1
    if size == 2:
        bm = pltpu.roll(b, 1, 0).reshape(shape3)
        x = jnp.where(upper, (b3 - bm) * LOG2E, 0.0)
    else:
        if size == 8:
            bm = jnp.broadcast_to(b3[:, 3:4, :], shape3)
        else:
            bm = jnp.where(sub < 4, jnp.broadcast_to(b3[:, 1:2, :], shape3),
                           jnp.broadcast_to(b3[:, 5:6, :], shape3))
        x = (b3 - bm) * jnp.where(upper, LOG2E, -LOG2E)
    a = jnp.where(upper, q3, k3) * jnp.exp2(x)
    return a.reshape(HG_CHUNK, HG_DIM).astype(BF16)


def _hgrn_rows(layer, q_ref, f_ref, i_ref, g_ref, lb_ref, nw_ref, tri_ref, lvl_ref,
               st, store, fillers):
    lbs = lb_ref[...]
    e = jnp.exp(lbs - jnp.max(lbs, axis=0, keepdims=True))
    soft = e / jnp.sum(e, axis=0, keepdims=True)
    lb = jnp.zeros((1, HG_DIM), F32)
    for r in range(1, layer + 1):
        lb = lb + soft[r:r + 1, :]
    lb = jnp.clip(lb, 0.0, 1.0)
    one_minus_lb = 1.0 - lb
    lb_tiny = lb + LOG_TINY
    nw = nw_ref[...]
    tri = tri_ref[...]
    pair_level = lvl_ref[...]
    n_chunks = q_ref.shape[0] // HG_CHUNK
    chunks = range(n_chunks)
    rows = [slice(c * HG_CHUNK, (c + 1) * HG_CHUNK) for c in chunks]
    pending = list(fillers)

    def fill():
        if pending:
            pending.pop(0)()

    q, k, b = [], [], []
    for c in chunks:
        zq = q_ref[rows[c], :].astype(F32)
        zf = f_ref[rows[c], :].astype(F32)
        q.append(zq * _sigmoid(zq) * (HG_DIM ** -0.5))
        sig = _sigmoid(zf)
        logf = jnp.log(lb_tiny + one_minus_lb * sig)
        k.append(one_minus_lb * (1.0 - sig) - LOG_TINY)
        p0 = logf.astype(BF16)
        r0 = logf - p0.astype(F32)
        p1 = r0.astype(BF16)
        p2 = (r0 - p1.astype(F32)).astype(BF16)
        cum3 = _dot(tri, jnp.concatenate([p0, p1, p2], axis=1))
        b.append(cum3[:, :HG_DIM] + cum3[:, HG_DIM:2 * HG_DIM] + cum3[:, 2 * HG_DIM:])
    fill()

    scores = []
    for c in chunks:
        fill()
        s = jnp.where(pair_level == 0, _dot_nt(q[c].astype(BF16), k[c].astype(BF16)), 0.0)
        for lvl in range(1, HG_LEVELS + 1):
            a = _level_operand(q[c], k[c], b[c], lvl)
            half = 1 << (lvl - 1)
            if half < SUBLANES:
                s = jnp.where(pair_level == lvl, _dot_nt(a, a), s)
                continue
            ups = [(lo + half, lo + 2 * half) for lo in range(0, HG_CHUNK, 2 * half)]
            s_up = _dot_nt(jnp.concatenate([a[m:h, :] for m, h in ups], axis=0), a)
            parts, at, row = [], 0, 0
            for m, h in ups:
                parts.append(s[row:m, :])
                parts.append(jnp.where(pair_level[m:h, :] == lvl, s_up[at:at + half, :], s[m:h, :]))
                at, row = at + half, h
            s = jnp.concatenate(parts, axis=0)
        scores.append(s.astype(BF16))
    fill()

    v = [i_ref[rows[c], :] for c in chunks]
    b_last = [b[c][HG_CHUNK - 1:HG_CHUNK, :] for c in chunks]
    kv = [_dot_tn(v[c], (k[c] * jnp.exp(b_last[c] - b[c])).astype(BF16)) for c in chunks]
    o_inter = []
    for c in chunks:
        o_inter.append(_dot_nt((q[c] * jnp.exp(b[c])).astype(BF16), st.astype(BF16)))
        st = st * jnp.exp(b_last[c]) + kv[c]
    fill()

    for c in chunks:
        o = _dot(scores[c], v[c]) + o_inter[c]
        zg = g_ref[rows[c], :].astype(F32)
        o = o * _rms_scale(o) * nw
        store(rows[c], (o * (zg * _sigmoid(zg))).astype(BF16))
    while pending:
        fill()
    return st


MERGE_SLAB = 256


def _mixer_kernel(layer, n_tiles, q_ref, f_ref, i_ref, g_ref, lb_ref, nw_ref, tri_ref, lvl_ref,
                  ya_ref, ym_ref, ga_ref, gb_ref, gm_ref, w_ref, o_ref, st_ref, yb_ref):
    t = pl.program_id(0)
    h = pl.program_id(1)
    tm = q_ref.shape[0]
    slot = lax.rem(t, 2)
    slab = min(MERGE_SLAB, tm)

    def run_hgrn(fillers):
        def store(rows, y):
            yb_ref[slot, h, rows, :] = y

        st_ref[h] = _hgrn_rows(layer, q_ref, f_ref, i_ref, g_ref, lb_ref, nw_ref, tri_ref, lvl_ref,
                               st_ref[h], store, fillers)

    def merge_slab(r):
        rows = slice(r * slab, (r + 1) * slab)
        acc = []

        def conv_branch():
            acc.append(_sigmoid(ga_ref[rows, :].astype(F32)) * _dot(ya_ref[rows, :], w_ref[0]))

        def hgrn_branch():
            yb = jnp.concatenate([yb_ref[1 - slot, hh, rows, :] for hh in range(HG_HEADS)], axis=1)
            acc.append(_sigmoid(gb_ref[rows, :].astype(F32)) * _dot(yb, w_ref[1]))

        def mem_branch():
            y = _sigmoid(gm_ref[rows, :].astype(F32)) * _dot(ym_ref[rows, :], w_ref[2])
            o_ref[rows, :] = (acc[0] + acc[1] + y).astype(o_ref.dtype)

        return [conv_branch, hgrn_branch, mem_branch]

    def merge_thunks():
        return [thunk for r in range(tm // slab) for thunk in merge_slab(r)]

    @pl.when(t == 0)
    def _():
        st_ref[h] = jnp.zeros(st_ref.shape[1:], F32)
        run_hgrn([])
        o_ref[...] = jnp.zeros_like(o_ref)

    @pl.when(jnp.logical_and(t > 0, t < n_tiles))
    def _():
        run_hgrn(merge_thunks())

    @pl.when(t == n_tiles)
    def _():
        for run in merge_thunks():
            run()


def _mixer(proj, lower_bounds, norm_w, w_branch, layer, tm, hg_col0, ya_col, ym_col, gate_col0):
    s = proj.shape[0]
    tm = min(tm, s)
    n_tiles = s // tm
    depth = lower_bounds.shape[0]
    c, d = w_branch.shape[2], w_branch.shape[3]
    assert c == HG_HEADS * HG_DIM and d % HG_HEADS == 0 and tm % HG_CHUNK == 0
    tn = d // HG_HEADS
    tri = jnp.asarray(np.tril(np.ones((HG_CHUNK, HG_CHUNK), np.float32)), BF16)
    pair_levels = jnp.asarray(_hgrn_pair_levels())

    def cur(t):
        return jnp.minimum(t, n_tiles - 1)

    def prev(t):
        return jnp.maximum(t - 1, 0)

    def head_col(k):
        return pl.BlockSpec((tm, HG_DIM), lambda t, h: (cur(t), hg_col0 // HG_DIM + k * HG_HEADS + h))

    def branch(col):
        return pl.BlockSpec((tm, c), lambda t, h: (prev(t), col // c))

    def gate(k):
        return pl.BlockSpec((tm, tn), lambda t, h: (prev(t), gate_col0 // tn + k * HG_HEADS + h))

    return pl.pallas_call(
        functools.partial(_mixer_kernel, layer, n_tiles),
        grid=(n_tiles + 1, HG_HEADS),
        in_specs=[head_col(0), head_col(1), head_col(2), head_col(3),
                  pl.BlockSpec((depth, HG_DIM), lambda t, h: (0, h)),
                  pl.BlockSpec((None, 1, HG_DIM), lambda t, h: (layer, 0, h)),
                  pl.BlockSpec((HG_CHUNK, HG_CHUNK), lambda t, h: (0, 0)),
                  pl.BlockSpec((HG_CHUNK, HG_CHUNK), lambda t, h: (0, 0)),
                  branch(ya_col), branch(ym_col), gate(0), gate(1), gate(2),
                  pl.BlockSpec((None, 3, c, tn), lambda t, h: (layer, 0, 0, h))],
        out_specs=pl.BlockSpec((tm, tn), lambda t, h: (jnp.where(t == 0, n_tiles, t - 1), h)),
        out_shape=jax.ShapeDtypeStruct((s + tm, d), BF16),
        scratch_shapes=[pltpu.VMEM((HG_HEADS, HG_DIM, HG_DIM), F32),
                        pltpu.VMEM((2, HG_HEADS, tm, HG_DIM), BF16)],
        compiler_params=_params(("arbitrary", "arbitrary")),
        name="mixer",
    )(proj, proj, proj, proj, lower_bounds, norm_w, tri, pair_levels,
      proj, proj, proj, proj, proj, w_branch)


def _proj_norm_res_single_kernel(a_ref, w_ref, x_ref, g_ref, o_ref):
    y = _dot(a_ref[...], w_ref[...])
    o_ref[...] = x_ref[...] + y * _rms_scale(y) * g_ref[...]


def _proj_norm_res_kernel(a_ref, w_ref, x_ref, g_ref, o_ref):
    kk = pl.program_id(1)

    @pl.when(kk == 0)
    def _():
        o_ref[...] = _dot(a_ref[...], w_ref[...])

    @pl.when(kk > 0)
    def _():
        o_ref[...] += _dot(a_ref[...], w_ref[...])

    @pl.when(kk == pl.num_programs(1) - 1)
    def _():
        rows = min(256, o_ref.shape[0])

        def body(r, _):
            sl = pl.ds(pl.multiple_of(r * rows, rows), rows)
            acc = o_ref[sl, :]
            o_ref[sl, :] = x_ref[sl, :] + acc * _rms_scale(acc) * g_ref[...]
            return 0

        lax.fori_loop(0, o_ref.shape[0] // rows, body, 0)


def _proj_norm_res(a, w, x, g, layer, tm, tk):
    s, k = x.shape[0], a.shape[1]
    d = w.shape[2]
    tm, tk = min(tm, s), min(tk, k)
    return pl.pallas_call(
        _proj_norm_res_single_kernel if tk == k else _proj_norm_res_kernel,
        grid=(s // tm, k // tk),
        in_specs=[pl.BlockSpec((tm, tk), lambda i, kk: (i, kk)),
                  pl.BlockSpec((None, tk, d), lambda i, kk: (layer, kk, 0)),
                  pl.BlockSpec((tm, d), lambda i, kk: (i, 0)),
                  pl.BlockSpec((None, 1, d), lambda i, kk: (layer, 0, 0))],
        out_specs=pl.BlockSpec((tm, d), lambda i, kk: (i, 0)),
        out_shape=jax.ShapeDtypeStruct((s, d), F32),
        compiler_params=_params(("arbitrary", "arbitrary")),
        name="proj_norm_res",
    )(a, w, x, g)


def _gelu_tanh(x):
    return 0.5 * x * (1.0 + jnp.tanh(np.sqrt(2.0 / np.pi) * (x + 0.044715 * (x * x * x))))


def _ffn_up_kernel(x_ref, g_ref, wg_ref, wv_ref, cw_ref, cb_ref, o_ref, h_ref, halo_ref):
    i = pl.program_id(0)
    j = pl.program_id(1)

    @pl.when(j == 0)
    def _():
        _store_normed(x_ref, g_ref, h_ref, min(256, x_ref.shape[0]))

    @pl.when(i == 0)
    def _():
        halo_ref[j] = jnp.zeros(halo_ref.shape[1:], F32)

    tm = x_ref.shape[0]
    h = h_ref[...]
    up_g = _dot(h, wg_ref[...].astype(BF16))
    z = _conv3_rows(up_g, halo_ref[j], cw_ref[...]) + cb_ref[...]
    halo_ref[j] = up_g[tm - SUBLANES:, :]
    o_ref[...] = (_gelu_tanh(z) * _dot(h, wv_ref[...].astype(BF16))).astype(o_ref.dtype)


def _ffn_up(x, g, w_up, conv_w, conv_b, layer, tm, tn):
    s, d = x.shape
    f = conv_w.shape[2]
    tm, tn = min(tm, s), min(tn, f)
    nj = f // tn
    return pl.pallas_call(
        _ffn_up_kernel,
        grid=(s // tm, nj),
        in_specs=[pl.BlockSpec((tm, d), lambda i, j: (i, 0)),
                  pl.BlockSpec((None, 1, d), lambda i, j: (layer, 0, 0)),
                  pl.BlockSpec((None, d, tn), lambda i, j: (layer, 0, j)),
                  pl.BlockSpec((None, d, tn), lambda i, j: (layer, 0, nj + j)),
                  pl.BlockSpec((None, CONV_K, tn), lambda i, j: (layer, 0, j)),
                  pl.BlockSpec((None, 1, tn), lambda i, j: (layer, 0, j))],
        out_specs=pl.BlockSpec((tm, tn), lambda i, j: (i, j)),
        out_shape=jax.ShapeDtypeStruct((s, f), BF16),
        scratch_shapes=[pltpu.VMEM((tm, d), BF16), pltpu.VMEM((nj, SUBLANES, tn), F32)],
        compiler_params=_params(("arbitrary", "arbitrary")),
        name="ffn_up",
    )(x, g, w_up, w_up, conv_w, conv_b)


def kernel(x, mem, w_in, conv_mix_w, hg_lower_bounds, hg_norm_w, w_mem_kv, w_branch, w_out,
           norm_mix_pre, norm_mix_post, norm_mem, norm_ffn_pre, norm_ffn_post,
           w_ffn_up, conv_ffn_w, conv_ffn_b, w_ffn_down):
    bsz, s, d = x.shape
    depth = w_in.shape[0]
    conv_width = conv_mix_w.shape[2]
    hg_width = hg_norm_w.shape[1]
    mem_width = w_mem_kv.shape[2] // 2
    assert conv_width == hg_width == mem_width == IN_TILE
    ya_col = 0
    hg_col0 = IN_TILE
    ym_col = hg_col0 + 4 * hg_width
    gate_col0 = ym_col + mem_width
    lower_bounds = hg_lower_bounds.astype(F32)

    def rows(p):
        return p.reshape(depth, 1, p.shape[1])

    w_br_b, w_out_b, w_dn_b = (w.astype(BF16) for w in (w_branch, w_out, w_ffn_down))
    g_mix_pre, g_mix_post, g_mem, g_ffn_pre, g_ffn_post, hg_nw, ffn_b = (
        rows(p) for p in (norm_mix_pre, norm_mix_post, norm_mem, norm_ffn_pre, norm_ffn_post,
                          hg_norm_w, conv_ffn_b))

    outs = []
    for b in range(bsz):
        xb = x[b]
        mb = mem[b]
        for l in range(depth):
            mem_kv = _norm_proj(mb, g_mem, w_mem_kv, l, 256, 1024)
            proj = _in_proj(xb, g_mix_pre, w_in, conv_mix_w, mem_kv, l, 1024)
            merged = _mixer(proj, lower_bounds, hg_nw, w_br_b, l, 1024, hg_col0, ya_col, ym_col,
                            gate_col0)
            xb = _proj_norm_res(merged, w_out_b, xb, g_mix_post, l, 512, d)
            act = _ffn_up(xb, g_ffn_pre, w_ffn_up, conv_ffn_w, ffn_b, l, 1024, 512)
            xb = _proj_norm_res(act, w_dn_b, xb, g_ffn_post, l, 1024, 1408)
        outs.append(xb)
    return outs[0][None] if bsz == 1 else jnp.stack(outs)
```

```python
import functools

import numpy as np
import jax
import jax.numpy as jnp
from jax import lax
from jax.experimental import pallas as pl
from jax.experimental.pallas import tpu as pltpu

F32 = jnp.float32
BF16 = jnp.bfloat16

RMS_EPS = 1e-6
LOG_TINY = 1e-30
CONV_K = 3
HG_HEADS = 8
HG_DIM = 128
HG_CHUNK = 64
HG_LEVELS = 6
HG_GROUP = 16
LOG2E = 1.4426950408889634
MEM_HEADS = 4
SUBLANES = 8
VMEM_LIMIT = 56 * 1024 * 1024


def _params(semantics):
    return pltpu.CompilerParams(dimension_semantics=semantics, vmem_limit_bytes=VMEM_LIMIT)


def _dot(a, b):
    return jnp.dot(a, b, preferred_element_type=F32)


def _dot_nt(a, b):
    return lax.dot_general(a, b, (((1,), (1,)), ((), ())), preferred_element_type=F32)


def _dot_tn(a, b):
    return lax.dot_general(a, b, (((0,), (0,)), ((), ())), preferred_element_type=F32)


def _rms_scale(x):
    return lax.rsqrt(jnp.mean(x * x, axis=-1, keepdims=True) + RMS_EPS)


def _sigmoid(x):
    return 1.0 / (1.0 + jnp.exp2(x * -LOG2E))


def _store_normed(x_ref, g_ref, h_ref, rows):
    tm = x_ref.shape[0]

    def body(r, _):
        sl = pl.ds(pl.multiple_of(r * rows, rows), rows)
        x = x_ref[sl, :]
        h_ref[sl, :] = (x * _rms_scale(x) * g_ref[...]).astype(BF16)
        return 0

    lax.fori_loop(0, tm // rows, body, 0)


def _norm_proj_kernel(x_ref, g_ref, w_ref, o_ref, h_ref):
    @pl.when(pl.program_id(1) == 0)
    def _():
        _store_normed(x_ref, g_ref, h_ref, min(256, x_ref.shape[0]))

    o_ref[...] = _dot(h_ref[...], w_ref[...].astype(BF16)).astype(o_ref.dtype)


def _norm_proj(x, g, w, layer, tm, tn):
    s, d = x.shape
    n = w.shape[2]
    tm, tn = min(tm, s), min(tn, n)
    return pl.pallas_call(
        _norm_proj_kernel,
        grid=(s // tm, n // tn),
        in_specs=[pl.BlockSpec((tm, d), lambda i, j: (i, 0)),
                  pl.BlockSpec((None, 1, d), lambda i, j: (layer, 0, 0)),
                  pl.BlockSpec((None, d, tn), lambda i, j: (layer, 0, j))],
        out_specs=pl.BlockSpec((tm, tn), lambda i, j: (i, j)),
        out_shape=jax.ShapeDtypeStruct((s, n), BF16),
        scratch_shapes=[pltpu.VMEM((tm, d), BF16)],
        compiler_params=_params(("arbitrary", "arbitrary")),
        name="norm_proj",
    )(x, g, w)


def _conv3_rows(u, halo, w):
    tm = u.shape[0]
    u1 = pltpu.roll(u, 1, 0)
    u2 = pltpu.roll(u, 2, 0)
    y = u2 * w[0:1, :] + u1 * w[1:2, :] + u * w[2:3, :]
    head = jnp.concatenate([halo, u[0:SUBLANES, :]], axis=0)
    h1 = pltpu.roll(head, 1, 0)[SUBLANES:, :]
    h2 = pltpu.roll(head, 2, 0)[SUBLANES:, :]
    y_head = h2 * w[0:1, :] + h1 * w[1:2, :] + u[0:SUBLANES, :] * w[2:3, :]
    if tm == SUBLANES:
        return y_head
    return jnp.concatenate([y_head, y[SUBLANES:, :]], axis=0)


IN_TILE = 1024
IN_SKIP = 2


def _in_proj_kernel(x_ref, g_ref, w_ref, cw_ref, kv_ref, o_ref, h_ref, stash_ref, halo_ref):
    i = pl.program_id(0)
    j = pl.program_id(1)
    tm, tn = o_ref.shape
    hd = tn // MEM_HEADS

    @pl.when(j == 0)
    def _():
        _store_normed(x_ref, g_ref, h_ref, min(256, tm))

    @pl.when(jnp.logical_and(i == 0, j == 0))
    def _():
        halo_ref[...] = jnp.zeros_like(halo_ref)

    def proj(lo, width):
        return _dot(h_ref[...], w_ref[:, lo:lo + width].astype(BF16))

    @pl.when(j < 2)
    def _():
        stash_ref[j] = proj(0, tn).astype(BF16)

    @pl.when(j == 2)
    def _():
        for lo in range(0, tn, hd):
            cols = slice(lo, lo + hd)
            u = stash_ref[1, :, cols].astype(F32) * proj(lo, hd)
            y = _conv3_rows(u, halo_ref[:, cols], cw_ref[:, cols])
            halo_ref[:, cols] = u[tm - SUBLANES:, :]
            o_ref[:, cols] = (stash_ref[0, :, cols].astype(F32) * y).astype(o_ref.dtype)

    @pl.when(jnp.logical_and(j > 2, j != 7))
    def _():
        o_ref[...] = proj(0, tn).astype(o_ref.dtype)

    @pl.when(j == 7)
    def _():
        wb = w_ref[...].astype(BF16)
        slab = min(256, tm)
        heads = range(MEM_HEADS)
        probs = None
        for r in range(tm // slab + 1):
            if r < tm // slab:
                q = _dot(h_ref[r * slab:(r + 1) * slab, :], wb).astype(BF16)
                scores = [_dot_nt(q[:, h * hd:(h + 1) * hd], kv_ref[:, h * hd:(h + 1) * hd])
                          * (hd ** -0.5) for h in heads]
            if probs is not None:
                for h in heads:
                    v = kv_ref[:, tn + h * hd:tn + (h + 1) * hd]
                    o_ref[(r - 1) * slab:r * slab, h * hd:(h + 1) * hd] = (
                        _dot(probs[h], v).astype(o_ref.dtype))
            if r < tm // slab:
                probs = []
                for sc in scores:
                    p = jnp.exp(sc - jnp.max(sc, axis=-1, keepdims=True))
                    probs.append((p / jnp.sum(p, axis=-1, keepdims=True)).astype(BF16))


def _in_proj(x, g, w, conv_w, mem_kv, layer, tm):
    s, d = x.shape
    n = w.shape[2]
    tm, tn = min(tm, s), IN_TILE
    assert conv_w.shape[2] == tn and mem_kv.shape[1] == 2 * tn
    return pl.pallas_call(
        _in_proj_kernel,
        grid=(s // tm, n // tn),
        in_specs=[pl.BlockSpec((tm, d), lambda i, j: (i, 0)),
                  pl.BlockSpec((None, 1, d), lambda i, j: (layer, 0, 0)),
                  pl.BlockSpec((None, d, tn), lambda i, j: (layer, 0, j)),
                  pl.BlockSpec((None, CONV_K, tn), lambda i, j: (layer, 0, 0)),
                  pl.BlockSpec(mem_kv.shape, lambda i, j: (0, 0))],
        out_specs=pl.BlockSpec((tm, tn), lambda i, j: (i, jnp.maximum(j - IN_SKIP, 0))),
        out_shape=jax.ShapeDtypeStruct((s, n - IN_SKIP * tn), BF16),
        scratch_shapes=[pltpu.VMEM((tm, d), BF16), pltpu.VMEM((2, tm, tn), BF16),
                        pltpu.VMEM((SUBLANES, tn), F32)],
        compiler_params=_params(("arbitrary", "arbitrary")),
        name="in_proj",
    )(x, g, w, conv_w, mem_kv)


def _hgrn_pair_levels():
    i = np.arange(HG_CHUNK)[:, None]
    j = np.arange(HG_CHUNK)[None, :]
    lvl = np.full((HG_CHUNK, HG_CHUNK), HG_LEVELS + 1, np.int32)
    lvl[i == j] = 0
    for level in range(1, HG_LEVELS + 1):
        lvl[((i >> level) == (j >> level)) & (((i >> (level - 1)) & 1) == 1)
            & (((j >> (level - 1)) & 1) == 0)] = level
    return lvl


def _level_operand(q, k, b, lvl):
    size = 1 << lvl
    half = size >> 1
    if half >= SUBLANES:
        parts = []
        for lo in range(0, HG_CHUNK, size):
            mid, hi = lo + half, lo + size
            bm = b[mid - 1:mid, :]
            parts.append(k[lo:mid, :] * jnp.exp(bm - b[lo:mid, :]))
            parts.append(q[mid:hi, :] * jnp.exp(b[mid:hi, :] - bm))
        return jnp.concatenate(parts, axis=0).astype(BF16)
    groups = HG_CHUNK // SUBLANES
    shape3 = (groups, SUBLANES, HG_DIM)
    b3, q3, k3 = b.reshape(shape3), q.reshape(shape3), k.reshape(shape3)
    sub = lax.broadcasted_iota(jnp.int32, (1, SUBLANES, HG_DIM), 1)
    upper = ((sub >> (lvl - 1)) & 1) == 1
    if size == 2:
        bm = pltpu.roll(b, 1, 0).reshape(shape3)
        x = jnp.where(upper, (b3 - bm) * LOG2E, 0.0)
    else:
        if size == 8:
            bm = jnp.broadcast_to(b3[:, 3:4, :], shape3)
        else:
            bm = jnp.where(sub < 4, jnp.broadcast_to(b3[:, 1:2, :], shape3),
                           jnp.broadcast_to(b3[:, 5:6, :], shape3))
        x = (b3 - bm) * jnp.where(upper, LOG2E, -LOG2E)
    a = jnp.where(upper, q3, k3) * jnp.exp2(x)
    return a.reshape(HG_CHUNK, HG_DIM).astype(BF16)


def _hgrn_kernel(layer, q_ref, f_ref, i_ref, g_ref, lb_ref, nw_ref, tri_ref, lvl_ref,
                 o_ref, st_ref):
    @pl.when(pl.program_id(1) == 0)
    def _():
        st_ref[...] = jnp.zeros_like(st_ref)

    lbs = lb_ref[...]
    e = jnp.exp(lbs - jnp.max(lbs, axis=0, keepdims=True))
    soft = e / jnp.sum(e, axis=0, keepdims=True)
    lb = jnp.zeros((1, HG_DIM), F32)
    for r in range(1, layer + 1):
        lb = lb + soft[r:r + 1, :]
    lb = jnp.clip(lb, 0.0, 1.0)
    one_minus_lb = 1.0 - lb
    lb_tiny = lb + LOG_TINY
    nw = nw_ref[...]
    tri = tri_ref[...]
    pair_level = lvl_ref[...]
    chunks = range(HG_GROUP)

    def group(gi, _):
        rows = [pl.ds(pl.multiple_of((gi * HG_GROUP + c) * HG_CHUNK, HG_CHUNK), HG_CHUNK)
                for c in chunks]

        q, k, b = [], [], []
        for c in chunks:
            zq = q_ref[rows[c], :].astype(F32)
            zf = f_ref[rows[c], :].astype(F32)
            q.append(zq * _sigmoid(zq) * (HG_DIM ** -0.5))
            sig = _sigmoid(zf)
            logf = jnp.log(lb_tiny + one_minus_lb * sig)
            k.append(one_minus_lb * (1.0 - sig) - LOG_TINY)
            p0 = logf.astype(BF16)
            r0 = logf - p0.astype(F32)
            p1 = r0.astype(BF16)
            p2 = (r0 - p1.astype(F32)).astype(BF16)
            cum3 = _dot(tri, jnp.concatenate([p0, p1, p2], axis=1))
            b.append(cum3[:, :HG_DIM] + cum3[:, HG_DIM:2 * HG_DIM] + cum3[:, 2 * HG_DIM:])

        scores = []
        for c in chunks:
            s = jnp.where(pair_level == 0, _dot_nt(q[c].astype(BF16), k[c].astype(BF16)), 0.0)
            for lvl in range(1, HG_LEVELS + 1):
                a = _level_operand(q[c], k[c], b[c], lvl)
                half = 1 << (lvl - 1)
                if half < SUBLANES:
                    s = jnp.where(pair_level == lvl, _dot_nt(a, a), s)
                    continue
                ups = [(lo + half, lo + 2 * half) for lo in range(0, HG_CHUNK, 2 * half)]
                s_up = _dot_nt(jnp.concatenate([a[m:h, :] for m, h in ups], axis=0), a)
                parts, at, row = [], 0, 0
                for m, h in ups:
                    parts.append(s[row:m, :])
                    parts.append(jnp.where(pair_level[m:h, :] == lvl, s_up[at:at + half, :], s[m:h, :]))
                    at, row = at + half, h
                s = jnp.concatenate(parts, axis=0)
            scores.append(s.astype(BF16))

        v = [i_ref[rows[c], :] for c in chunks]
        b_last = [b[c][HG_CHUNK - 1:HG_CHUNK, :] for c in chunks]
        kv = [_dot_tn(v[c], (k[c] * jnp.exp(b_last[c] - b[c])).astype(BF16)) for c in chunks]
        st = st_ref[...]
        o_inter = []
        for c in chunks:
            o_inter.append(_dot_nt((q[c] * jnp.exp(b[c])).astype(BF16), st.astype(BF16)))
            st = st * jnp.exp(b_last[c]) + kv[c]
        st_ref[...] = st

        for c in chunks:
            o = _dot(scores[c], v[c]) + o_inter[c]
            zg = g_ref[rows[c], :].astype(F32)
            o = o * _rms_scale(o) * nw
            o_ref[rows[c], :] = (o * (zg * _sigmoid(zg))).astype(o_ref.dtype)
        return 0

    lax.fori_loop(0, q_ref.shape[0] // (HG_CHUNK * HG_GROUP), group, 0)


def _hgrn(proj, lower_bounds, norm_w, layer, tm, col0):
    s = proj.shape[0]
    tm = min(tm, s)
    assert tm % (HG_CHUNK * HG_GROUP) == 0
    depth = lower_bounds.shape[0]
    width = HG_HEADS * HG_DIM
    cb = col0 // HG_DIM
    tri = jnp.asarray(np.tril(np.ones((HG_CHUNK, HG_CHUNK), np.float32)), BF16)
    pair_levels = jnp.asarray(_hgrn_pair_levels())

    def col(k):
        return pl.BlockSpec((tm, HG_DIM), lambda h, i: (i, cb + k * HG_HEADS + h))

    return pl.pallas_call(
        functools.partial(_hgrn_kernel, layer),
        grid=(HG_HEADS, s // tm),
        in_specs=[col(0), col(1), col(2), col(3),
                  pl.BlockSpec((depth, HG_DIM), lambda h, i: (0, h)),
                  pl.BlockSpec((None, 1, HG_DIM), lambda h, i: (layer, 0, h)),
                  pl.BlockSpec((HG_CHUNK, HG_CHUNK), lambda h, i: (0, 0)),
                  pl.BlockSpec((HG_CHUNK, HG_CHUNK), lambda h, i: (0, 0))],
        out_specs=pl.BlockSpec((tm, HG_DIM), lambda h, i: (i, h)),
        out_shape=jax.ShapeDtypeStruct((s, width), BF16),
        scratch_shapes=[pltpu.VMEM((HG_DIM, HG_DIM), F32)],
        compiler_params=_params(("arbitrary", "arbitrary")),
        name="hgrn2",
    )(proj, proj, proj, proj, lower_bounds, norm_w, tri, pair_levels)


def _merge_kernel(ya_ref, yb_ref, ym_ref, ga_ref, gb_ref, gm_ref, w_ref, o_ref):
    acc = _sigmoid(ga_ref[...].astype(F32)) * _dot(ya_ref[...], w_ref[0])
    acc = acc + _sigmoid(gb_ref[...].astype(F32)) * _dot(yb_ref[...], w_ref[1])
    acc = acc + _sigmoid(gm_ref[...].astype(F32)) * _dot(ym_ref[...], w_ref[2])
    o_ref[...] = acc.astype(o_ref.dtype)


def _merge(proj, y_b, w_branch, layer, tm, tn, ya_col, ym_col, col0):
    s, c = y_b.shape
    d = w_branch.shape[3]
    tm, tn = min(tm, s), min(tn, d)
    gcb = col0 // tn
    nj = d // tn

    def gate(k):
        return pl.BlockSpec((tm, tn), lambda j, i: (i, gcb + k * nj + j))

    def branch(col):
        return pl.BlockSpec((tm, c), lambda j, i: (i, col // c))

    return pl.pallas_call(
        _merge_kernel,
        grid=(nj, s // tm),
        in_specs=[branch(ya_col), branch(0), branch(ym_col), gate(0), gate(1), gate(2),
                  pl.BlockSpec((None, 3, c, tn), lambda j, i: (layer, 0, 0, j))],
        out_specs=pl.BlockSpec((tm, tn), lambda j, i: (i, j)),
        out_shape=jax.ShapeDtypeStruct((s, d), BF16),
        compiler_params=_params(("arbitrary", "arbitrary")),
        name="merge",
    )(proj, y_b, proj, proj, proj, proj, w_branch)


def _proj_norm_res_single_kernel(a_ref, w_ref, x_ref, g_ref, o_ref):
    y = _dot(a_ref[...], w_ref[...].astype(BF16))
    o_ref[...] = x_ref[...] + y * _rms_scale(y) * g_ref[...]


def _proj_norm_res_kernel(a_ref, w_ref, x_ref, g_ref, o_ref):
    kk = pl.program_id(1)

    @pl.when(kk == 0)
    def _():
        o_ref[...] = _dot(a_ref[...], w_ref[...])

    @pl.when(kk > 0)
    def _():
        o_ref[...] += _dot(a_ref[...], w_ref[...])

    @pl.when(kk == pl.num_programs(1) - 1)
    def _():
        rows = min(256, o_ref.shape[0])

        def body(r, _):
            sl = pl.ds(pl.multiple_of(r * rows, rows), rows)
            acc = o_ref[sl, :]
            o_ref[sl, :] = x_ref[sl, :] + acc * _rms_scale(acc) * g_ref[...]
            return 0

        lax.fori_loop(0, o_ref.shape[0] // rows, body, 0)


def _proj_norm_res(a, w, x, g, layer, tm, tk):
    s, k = a.shape
    d = w.shape[2]
    tm, tk = min(tm, s), min(tk, k)
    return pl.pallas_call(
        _proj_norm_res_single_kernel if tk == k else _proj_norm_res_kernel,
        grid=(s // tm, k // tk),
        in_specs=[pl.BlockSpec((tm, tk), lambda i, kk: (i, kk)),
                  pl.BlockSpec((None, tk, d), lambda i, kk: (layer, kk, 0)),
                  pl.BlockSpec((tm, d), lambda i, kk: (i, 0)),
                  pl.BlockSpec((None, 1, d), lambda i, kk: (layer, 0, 0))],
        out_specs=pl.BlockSpec((tm, d), lambda i, kk: (i, 0)),
        out_shape=jax.ShapeDtypeStruct((s, d), F32),
        compiler_params=_params(("arbitrary", "arbitrary")),
        name="proj_norm_res",
    )(a, w, x, g)


def _gelu_tanh(x):
    return 0.5 * x * (1.0 + jnp.tanh(np.sqrt(2.0 / np.pi) * (x + 0.044715 * (x * x * x))))


def _ffn_up_kernel(x_ref, g_ref, wg_ref, wv_ref, cw_ref, cb_ref, o_ref, h_ref, halo_ref):
    i = pl.program_id(0)
    j = pl.program_id(1)

    @pl.when(j == 0)
    def _():
        _store_normed(x_ref, g_ref, h_ref, min(256, x_ref.shape[0]))

    @pl.when(i == 0)
    def _():
        halo_ref[j] = jnp.zeros(halo_ref.shape[1:], F32)

    tm = x_ref.shape[0]
    h = h_ref[...]
    up_g = _dot(h, wg_ref[...].astype(BF16))
    z = _conv3_rows(up_g, halo_ref[j], cw_ref[...]) + cb_ref[...]
    halo_ref[j] = up_g[tm - SUBLANES:, :]
    o_ref[...] = (_gelu_tanh(z) * _dot(h, wv_ref[...].astype(BF16))).astype(o_ref.dtype)


def _ffn_up(x, g, w_up, conv_w, conv_b, layer, tm, tn):
    s, d = x.shape
    f = conv_w.shape[2]
    tm, tn = min(tm, s), min(tn, f)
    nj = f // tn
    return pl.pallas_call(
        _ffn_up_kernel,
        grid=(s // tm, nj),
        in_specs=[pl.BlockSpec((tm, d), lambda i, j: (i, 0)),
                  pl.BlockSpec((None, 1, d), lambda i, j: (layer, 0, 0)),
                  pl.BlockSpec((None, d, tn), lambda i, j: (layer, 0, j)),
                  pl.BlockSpec((None, d, tn), lambda i, j: (layer, 0, nj + j)),
                  pl.BlockSpec((None, CONV_K, tn), lambda i, j: (layer, 0, j)),
                  pl.BlockSpec((None, 1, tn), lambda i, j: (layer, 0, j))],
        out_specs=pl.BlockSpec((tm, tn), lambda i, j: (i, j)),
        out_shape=jax.ShapeDtypeStruct((s, f), BF16),
        scratch_shapes=[pltpu.VMEM((tm, d), BF16), pltpu.VMEM((nj, SUBLANES, tn), F32)],
        compiler_params=_params(("arbitrary", "arbitrary")),
        name="ffn_up",
    )(x, g, w_up, w_up, conv_w, conv_b)


def kernel(x, mem, w_in, conv_mix_w, hg_lower_bounds, hg_norm_w, w_mem_kv, w_branch, w_out,
           norm_mix_pre, norm_mix_post, norm_mem, norm_ffn_pre, norm_ffn_post,
           w_ffn_up, conv_ffn_w, conv_ffn_b, w_ffn_down):
    bsz, s, d = x.shape
    depth = w_in.shape[0]
    conv_width = conv_mix_w.shape[2]
    hg_width = hg_norm_w.shape[1]
    mem_width = w_mem_kv.shape[2] // 2
    assert conv_width == hg_width == mem_width == IN_TILE
    ya_col = 0
    hg_col0 = IN_TILE
    ym_col = hg_col0 + 4 * hg_width
    gate_col0 = ym_col + mem_width
    lower_bounds = hg_lower_bounds.astype(F32)

    def rows(p):
        return p.reshape(depth, 1, p.shape[1])

    w_br_b, w_dn_b = w_branch.astype(BF16), w_ffn_down.astype(BF16)
    g_mix_pre, g_mix_post, g_mem, g_ffn_pre, g_ffn_post, hg_nw, ffn_b = (
        rows(p) for p in (norm_mix_pre, norm_mix_post, norm_mem, norm_ffn_pre, norm_ffn_post,
                          hg_norm_w, conv_ffn_b))

    outs = []
    for b in range(bsz):
        xb = x[b]
        mb = mem[b]
        for l in range(depth):
            mem_kv = _norm_proj(mb, g_mem, w_mem_kv, l, 256, 1024)
            proj = _in_proj(xb, g_mix_pre, w_in, conv_mix_w, mem_kv, l, 1024)
            y_b = _hgrn(proj, lower_bounds, hg_nw, l, 1024, hg_col0)
            merged = _merge(proj, y_b, w_br_b, l, 1024, 1024, ya_col, ym_col, gate_col0)
            xb = _proj_norm_res(merged, w_out, xb, g_mix_post, l, 512, d)
            act = _ffn_up(xb, g_ffn_pre, w_ffn_up, conv_ffn_w, ffn_b, l, 1024, 512)
            xb = _proj_norm_res(act, w_dn_b, xb, g_ffn_post, l, 1024, 1408)
        outs.append(xb)
    return outs[0][None] if bsz == 1 else jnp.stack(outs)
```

```python
import functools

import numpy as np
import jax
import jax.numpy as jnp
from jax import lax
from jax.experimental import pallas as pl
from jax.experimental.pallas import tpu as pltpu

F32 = jnp.float32
BF16 = jnp.bfloat16

RMS_EPS = 1e-6
LOG_TINY = 1e-30
CONV_K = 3
HG_HEADS = 8
HG_DIM = 128
HG_CHUNK = 64
HG_LEVELS = 6
HG_GROUP = 16
LOG2E = 1.4426950408889634
MEM_HEADS = 4
SUBLANES = 8
VMEM_LIMIT = 56 * 1024 * 1024

ROW_TILE = 1024
OUT_PROJ_ROWS = 512
FFN_TILE = 512
FFN_DOWN_K = 1408
MEM_ROWS = 256
ROW_SLAB = 256
HGRN_ROWS = 2 * HG_GROUP * HG_CHUNK


def _params(semantics):
    return pltpu.CompilerParams(dimension_semantics=semantics, vmem_limit_bytes=VMEM_LIMIT)


def _dot(a, b):
    return jnp.dot(a, b, preferred_element_type=F32)


def _dot_nt(a, b):
    return lax.dot_general(a, b, (((1,), (1,)), ((), ())), preferred_element_type=F32)


def _dot_tn(a, b):
    return lax.dot_general(a, b, (((0,), (0,)), ((), ())), preferred_element_type=F32)


def _rms_scale(x):
    return lax.rsqrt(jnp.mean(x * x, axis=-1, keepdims=True) + RMS_EPS)


def _sigmoid(x):
    return 1.0 / (1.0 + jnp.exp2(x * -LOG2E))


def _store_normed(x_ref, g_ref, h_ref, rows):
    tm = x_ref.shape[0]

    def body(r, _):
        sl = pl.ds(pl.multiple_of(r * rows, rows), rows)
        x = x_ref[sl, :]
        h_ref[sl, :] = (x * _rms_scale(x) * g_ref[...]).astype(BF16)
        return 0

    lax.fori_loop(0, tm // rows, body, 0)


def _norm_proj_kernel(x_ref, g_ref, w_ref, o_ref, h_ref):
    @pl.when(pl.program_id(1) == 0)
    def _():
        _store_normed(x_ref, g_ref, h_ref, min(ROW_SLAB, x_ref.shape[0]))

    o_ref[...] = _dot(h_ref[...], w_ref[...].astype(BF16)).astype(o_ref.dtype)


def _norm_proj(x, g, w, layer, tm, tn):
    s, d = x.shape
    n = w.shape[2]
    tm, tn = min(tm, s), min(tn, n)
    return pl.pallas_call(
        _norm_proj_kernel,
        grid=(s // tm, n // tn),
        in_specs=[pl.BlockSpec((tm, d), lambda i, j: (i, 0)),
                  pl.BlockSpec((None, 1, d), lambda i, j: (layer, 0, 0)),
                  pl.BlockSpec((None, d, tn), lambda i, j: (layer, 0, j))],
        out_specs=pl.BlockSpec((tm, tn), lambda i, j: (i, j)),
        out_shape=jax.ShapeDtypeStruct((s, n), BF16),
        scratch_shapes=[pltpu.VMEM((tm, d), BF16)],
        compiler_params=_params(("arbitrary", "arbitrary")),
        name="norm_proj",
    )(x, g, w)


def _conv3_rows(u, halo, w):
    tm = u.shape[0]
    u1 = pltpu.roll(u, 1, 0)
    u2 = pltpu.roll(u, 2, 0)
    y = u2 * w[0:1, :] + u1 * w[1:2, :] + u * w[2:3, :]
    head = jnp.concatenate([halo, u[0:SUBLANES, :]], axis=0)
    h1 = pltpu.roll(head, 1, 0)[SUBLANES:, :]
    h2 = pltpu.roll(head, 2, 0)[SUBLANES:, :]
    y_head = h2 * w[0:1, :] + h1 * w[1:2, :] + u[0:SUBLANES, :] * w[2:3, :]
    if tm == SUBLANES:
        return y_head
    return jnp.concatenate([y_head, y[SUBLANES:, :]], axis=0)


IN_TILE = 1024
IN_SKIP = 2


def _in_proj_kernel(x_ref, g_ref, w_ref, cw_ref, kv_ref, o_ref, h_ref, stash_ref, halo_ref):
    i = pl.program_id(0)
    j = pl.program_id(1)
    tm, tn = o_ref.shape
    hd = tn // MEM_HEADS

    @pl.when(j == 0)
    def _():
        _store_normed(x_ref, g_ref, h_ref, min(ROW_SLAB, tm))

    @pl.when(jnp.logical_and(i == 0, j == 0))
    def _():
        halo_ref[...] = jnp.zeros_like(halo_ref)

    def proj(lo, width):
        return _dot(h_ref[...], w_ref[:, lo:lo + width].astype(BF16))

    @pl.when(j < 2)
    def _():
        stash_ref[j] = proj(0, tn).astype(BF16)

    @pl.when(j == 2)
    def _():
        for lo in range(0, tn, hd):
            cols = slice(lo, lo + hd)
            u = stash_ref[1, :, cols].astype(F32) * proj(lo, hd)
            y = _conv3_rows(u, halo_ref[:, cols], cw_ref[:, cols])
            halo_ref[:, cols] = u[tm - SUBLANES:, :]
            o_ref[:, cols] = (stash_ref[0, :, cols].astype(F32) * y).astype(o_ref.dtype)

    @pl.when(jnp.logical_and(j > 2, j != 7))
    def _():
        o_ref[...] = proj(0, tn).astype(o_ref.dtype)

    @pl.when(j == 7)
    def _():
        wb = w_ref[...].astype(BF16)
        slab = min(ROW_SLAB, tm)
        heads = range(MEM_HEADS)
        probs = None
        for r in range(tm // slab + 1):
            if r < tm // slab:
                q = _dot(h_ref[r * slab:(r + 1) * slab, :], wb).astype(BF16)
                scores = [_dot_nt(q[:, h * hd:(h + 1) * hd], kv_ref[:, h * hd:(h + 1) * hd])
                          * (hd ** -0.5) for h in heads]
            if probs is not None:
                for h in heads:
                    v = kv_ref[:, tn + h * hd:tn + (h + 1) * hd]
                    o_ref[(r - 1) * slab:r * slab, h * hd:(h + 1) * hd] = (
                        _dot(probs[h], v).astype(o_ref.dtype))
            if r < tm // slab:
                probs = []
                for sc in scores:
                    p = jnp.exp(sc - jnp.max(sc, axis=-1, keepdims=True))
                    probs.append((p / jnp.sum(p, axis=-1, keepdims=True)).astype(BF16))


def _in_proj(x, g, w, conv_w, mem_kv, layer, tm):
    s, d = x.shape
    n = w.shape[2]
    tm, tn = min(tm, s), IN_TILE
    assert conv_w.shape[2] == tn and mem_kv.shape[1] == 2 * tn
    return pl.pallas_call(
        _in_proj_kernel,
        grid=(s // tm, n // tn),
        in_specs=[pl.BlockSpec((tm, d), lambda i, j: (i, 0)),
                  pl.BlockSpec((None, 1, d), lambda i, j: (layer, 0, 0)),
                  pl.BlockSpec((None, d, tn), lambda i, j: (layer, 0, j)),
                  pl.BlockSpec((None, CONV_K, tn), lambda i, j: (layer, 0, 0)),
                  pl.BlockSpec(mem_kv.shape, lambda i, j: (0, 0))],
        out_specs=pl.BlockSpec((tm, tn), lambda i, j: (i, jnp.maximum(j - IN_SKIP, 0))),
        out_shape=jax.ShapeDtypeStruct((s, n - IN_SKIP * tn), BF16),
        scratch_shapes=[pltpu.VMEM((tm, d), BF16), pltpu.VMEM((2, tm, tn), BF16),
                        pltpu.VMEM((SUBLANES, tn), F32)],
        compiler_params=_params(("arbitrary", "arbitrary")),
        name="in_proj",
    )(x, g, w, conv_w, mem_kv)


def _hgrn_pair_levels():
    i = np.arange(HG_CHUNK)[:, None]
    j = np.arange(HG_CHUNK)[None, :]
    lvl = np.full((HG_CHUNK, HG_CHUNK), HG_LEVELS + 1, np.int32)
    lvl[i == j] = 0
    for level in range(1, HG_LEVELS + 1):
        lvl[((i >> level) == (j >> level)) & (((i >> (level - 1)) & 1) == 1)
            & (((j >> (level - 1)) & 1) == 0)] = level
    return lvl


def _level_operand(q, k, b, lvl):
    size = 1 << lvl
    half = size >> 1
    if half >= SUBLANES:
        parts = []
        for lo in range(0, HG_CHUNK, size):
            mid, hi = lo + half, lo + size
            bm = b[mid - 1:mid, :]
            parts.append(k[lo:mid, :] * jnp.exp(bm - b[lo:mid, :]))
            parts.append(q[mid:hi, :] * jnp.exp(b[mid:hi, :] - bm))
        return jnp.concatenate(parts, axis=0).astype(BF16)
    groups = HG_CHUNK // SUBLANES
    shape3 = (groups, SUBLANES, HG_DIM)
    b3, q3, k3 = b.reshape(shape3), q.reshape(shape3), k.reshape(shape3)
    sub = lax.broadcasted_iota(jnp.int32, (1, SUBLANES, HG_DIM), 1)
    upper = ((sub >> (lvl - 1)) & 1) == 1
    if size == 2:
        bm = pltpu.roll(b, 1, 0).reshape(shape3)
        x = jnp.where(upper, (b3 - bm) * LOG2E, 0.0)
    else:
        if size == 8:
            bm = jnp.broadcast_to(b3[:, 3:4, :], shape3)
        else:
            bm = jnp.where(sub < 4, jnp.broadcast_to(b3[:, 1:2, :], shape3),
                           jnp.broadcast_to(b3[:, 5:6, :], shape3))
        x = (b3 - bm) * jnp.where(upper, LOG2E, -LOG2E)
    a = jnp.where(upper, q3, k3) * jnp.exp2(x)
    return a.reshape(HG_CHUNK, HG_DIM).astype(BF16)


def _hgrn_kernel(layer, q_ref, f_ref, i_ref, g_ref, lb_ref, nw_ref, tri_ref, lvl_ref,
                 o_ref, st_ref):
    @pl.when(pl.program_id(1) == 0)
    def _():
        st_ref[...] = jnp.zeros_like(st_ref)

    lbs = lb_ref[...]
    e = jnp.exp(lbs - jnp.max(lbs, axis=0, keepdims=True))
    soft = e / jnp.sum(e, axis=0, keepdims=True)
    lb = jnp.zeros((1, HG_DIM), F32)
    for r in range(1, layer + 1):
        lb = lb + soft[r:r + 1, :]
    lb = jnp.clip(lb, 0.0, 1.0)
    one_minus_lb = 1.0 - lb
    lb_tiny = lb + LOG_TINY
    nw = nw_ref[...]
    tri = tri_ref[...]
    pair_level = lvl_ref[...]
    chunks = range(HG_GROUP)

    def group(gi, _):
        rows = [pl.ds(pl.multiple_of((gi * HG_GROUP + c) * HG_CHUNK, HG_CHUNK), HG_CHUNK)
                for c in chunks]

        q, k, b = [], [], []
        for c in chunks:
            zq = q_ref[rows[c], :].astype(F32)
            zf = f_ref[rows[c], :].astype(F32)
            q.append(zq * _sigmoid(zq) * (HG_DIM ** -0.5))
            sig = _sigmoid(zf)
            logf = jnp.log(lb_tiny + one_minus_lb * sig)
            k.append(one_minus_lb * (1.0 - sig) - LOG_TINY)
            p0 = logf.astype(BF16)
            r0 = logf - p0.astype(F32)
            p1 = r0.astype(BF16)
            p2 = (r0 - p1.astype(F32)).astype(BF16)
            cum3 = _dot(tri, jnp.concatenate([p0, p1, p2], axis=1))
            b.append(cum3[:, :HG_DIM] + cum3[:, HG_DIM:2 * HG_DIM] + cum3[:, 2 * HG_DIM:])

        scores = []
        for c in chunks:
            s = jnp.where(pair_level == 0, _dot_nt(q[c].astype(BF16), k[c].astype(BF16)), 0.0)
            for lvl in range(1, HG_LEVELS + 1):
                a = _level_operand(q[c], k[c], b[c], lvl)
                half = 1 << (lvl - 1)
                if half < SUBLANES:
                    s = jnp.where(pair_level == lvl, _dot_nt(a, a), s)
                    continue
                ups = [(lo + half, lo + 2 * half) for lo in range(0, HG_CHUNK, 2 * half)]
                s_up = _dot_nt(jnp.concatenate([a[m:h, :] for m, h in ups], axis=0), a)
                parts, at, row = [], 0, 0
                for m, h in ups:
                    parts.append(s[row:m, :])
                    parts.append(jnp.where(pair_level[m:h, :] == lvl, s_up[at:at + half, :], s[m:h, :]))
                    at, row = at + half, h
                s = jnp.concatenate(parts, axis=0)
            scores.append(s.astype(BF16))

        v = [i_ref[rows[c], :] for c in chunks]
        b_last = [b[c][HG_CHUNK - 1:HG_CHUNK, :] for c in chunks]
        kv = [_dot_tn(v[c], (k[c] * jnp.exp(b_last[c] - b[c])).astype(BF16)) for c in chunks]
        st = st_ref[...]
        o_inter = []
        for c in chunks:
            o_inter.append(_dot_nt((q[c] * jnp.exp(b[c])).astype(BF16), st.astype(BF16)))
            st = st * jnp.exp(b_last[c]) + kv[c]
        st_ref[...] = st

        for c in chunks:
            o = _dot(scores[c], v[c]) + o_inter[c]
            zg = g_ref[rows[c], :].astype(F32)
            o = o * _rms_scale(o) * nw
            o_ref[rows[c], :] = (o * (zg * _sigmoid(zg))).astype(o_ref.dtype)
        return 0

    lax.fori_loop(0, q_ref.shape[0] // (HG_CHUNK * HG_GROUP), group, 0)


def _hgrn(proj, lower_bounds, norm_w, layer, tm, col0):
    s = proj.shape[0]
    tm = min(tm, s)
    assert tm % (HG_CHUNK * HG_GROUP) == 0
    depth = lower_bounds.shape[0]
    width = HG_HEADS * HG_DIM
    cb = col0 // HG_DIM
    tri = jnp.asarray(np.tril(np.ones((HG_CHUNK, HG_CHUNK), np.float32)), BF16)
    pair_levels = jnp.asarray(_hgrn_pair_levels())

    def col(k):
        return pl.BlockSpec((tm, HG_DIM), lambda h, i: (i, cb + k * HG_HEADS + h))

    return pl.pallas_call(
        functools.partial(_hgrn_kernel, layer),
        grid=(HG_HEADS, s // tm),
        in_specs=[col(0), col(1), col(2), col(3),
                  pl.BlockSpec((depth, HG_DIM), lambda h, i: (0, h)),
                  pl.BlockSpec((None, 1, HG_DIM), lambda h, i: (layer, 0, h)),
                  pl.BlockSpec((HG_CHUNK, HG_CHUNK), lambda h, i: (0, 0)),
                  pl.BlockSpec((HG_CHUNK, HG_CHUNK), lambda h, i: (0, 0))],
        out_specs=pl.BlockSpec((tm, HG_DIM), lambda h, i: (i, h)),
        out_shape=jax.ShapeDtypeStruct((s, width), BF16),
        scratch_shapes=[pltpu.VMEM((HG_DIM, HG_DIM), F32)],
        compiler_params=_params(("arbitrary", "arbitrary")),
        name="hgrn2",
    )(proj, proj, proj, proj, lower_bounds, norm_w, tri, pair_levels)


def _merge_kernel(ya_ref, yb_ref, ym_ref, ga_ref, gb_ref, gm_ref, w_ref, o_ref):
    acc = _sigmoid(ga_ref[...].astype(F32)) * _dot(ya_ref[...], w_ref[0])
    acc = acc + _sigmoid(gb_ref[...].astype(F32)) * _dot(yb_ref[...], w_ref[1])
    acc = acc + _sigmoid(gm_ref[...].astype(F32)) * _dot(ym_ref[...], w_ref[2])
    o_ref[...] = acc.astype(o_ref.dtype)


def _merge(proj, y_b, w_branch, layer, tm, tn, ya_col, ym_col, col0):
    s, c = y_b.shape
    d = w_branch.shape[3]
    tm, tn = min(tm, s), min(tn, d)
    gcb = col0 // tn
    nj = d // tn

    def gate(k):
        return pl.BlockSpec((tm, tn), lambda j, i: (i, gcb + k * nj + j))

    def branch(col):
        return pl.BlockSpec((tm, c), lambda j, i: (i, col // c))

    return pl.pallas_call(
        _merge_kernel,
        grid=(nj, s // tm),
        in_specs=[branch(ya_col), branch(0), branch(ym_col), gate(0), gate(1), gate(2),
                  pl.BlockSpec((None, 3, c, tn), lambda j, i: (layer, 0, 0, j))],
        out_specs=pl.BlockSpec((tm, tn), lambda j, i: (i, j)),
        out_shape=jax.ShapeDtypeStruct((s, d), BF16),
        compiler_params=_params(("arbitrary", "arbitrary")),
        name="merge",
    )(proj, y_b, proj, proj, proj, proj, w_branch)


def _proj_norm_res_single_kernel(a_ref, w_ref, x_ref, g_ref, o_ref):
    y = _dot(a_ref[...], w_ref[...].astype(BF16))
    o_ref[...] = x_ref[...] + y * _rms_scale(y) * g_ref[...]


def _proj_norm_res_kernel(a_ref, w_ref, x_ref, g_ref, o_ref):
    kk = pl.program_id(1)

    @pl.when(kk == 0)
    def _():
        o_ref[...] = _dot(a_ref[...], w_ref[...])

    @pl.when(kk > 0)
    def _():
        o_ref[...] += _dot(a_ref[...], w_ref[...])

    @pl.when(kk == pl.num_programs(1) - 1)
    def _():
        rows = min(ROW_SLAB, o_ref.shape[0])

        def body(r, _):
            sl = pl.ds(pl.multiple_of(r * rows, rows), rows)
            acc = o_ref[sl, :]
            o_ref[sl, :] = x_ref[sl, :] + acc * _rms_scale(acc) * g_ref[...]
            return 0

        lax.fori_loop(0, o_ref.shape[0] // rows, body, 0)


def _proj_norm_res(a, w, x, g, layer, tm, tk):
    s, k = a.shape
    d = w.shape[2]
    tm, tk = min(tm, s), min(tk, k)
    return pl.pallas_call(
        _proj_norm_res_single_kernel if tk == k else _proj_norm_res_kernel,
        grid=(s // tm, k // tk),
        in_specs=[pl.BlockSpec((tm, tk), lambda i, kk: (i, kk)),
                  pl.BlockSpec((None, tk, d), lambda i, kk: (layer, kk, 0)),
                  pl.BlockSpec((tm, d), lambda i, kk: (i, 0)),
                  pl.BlockSpec((None, 1, d), lambda i, kk: (layer, 0, 0))],
        out_specs=pl.BlockSpec((tm, d), lambda i, kk: (i, 0)),
        out_shape=jax.ShapeDtypeStruct((s, d), F32),
        compiler_params=_params(("arbitrary", "arbitrary")),
        name="proj_norm_res",
    )(a, w, x, g)


def _gelu_tanh(x):
    return 0.5 * x * (1.0 + jnp.tanh(np.sqrt(2.0 / np.pi) * (x + 0.044715 * (x * x * x))))


def _ffn_up_kernel(x_ref, g_ref, wg_ref, wv_ref, cw_ref, cb_ref, o_ref, h_ref, halo_ref):
    i = pl.program_id(0)
    j = pl.program_id(1)

    @pl.when(j == 0)
    def _():
        _store_normed(x_ref, g_ref, h_ref, min(ROW_SLAB, x_ref.shape[0]))

    @pl.when(i == 0)
    def _():
        halo_ref[j] = jnp.zeros(halo_ref.shape[1:], F32)

    tm = x_ref.shape[0]
    h = h_ref[...]
    up_g = _dot(h, wg_ref[...].astype(BF16))
    z = _conv3_rows(up_g, halo_ref[j], cw_ref[...]) + cb_ref[...]
    halo_ref[j] = up_g[tm - SUBLANES:, :]
    o_ref[...] = (_gelu_tanh(z) * _dot(h, wv_ref[...].astype(BF16))).astype(o_ref.dtype)


def _ffn_up(x, g, w_up, conv_w, conv_b, layer, tm, tn):
    s, d = x.shape
    f = conv_w.shape[2]
    tm, tn = min(tm, s), min(tn, f)
    nj = f // tn
    return pl.pallas_call(
        _ffn_up_kernel,
        grid=(s // tm, nj),
        in_specs=[pl.BlockSpec((tm, d), lambda i, j: (i, 0)),
                  pl.BlockSpec((None, 1, d), lambda i, j: (layer, 0, 0)),
                  pl.BlockSpec((None, d, tn), lambda i, j: (layer, 0, j)),
                  pl.BlockSpec((None, d, tn), lambda i, j: (layer, 0, nj + j)),
                  pl.BlockSpec((None, CONV_K, tn), lambda i, j: (layer, 0, j)),
                  pl.BlockSpec((None, 1, tn), lambda i, j: (layer, 0, j))],
        out_specs=pl.BlockSpec((tm, tn), lambda i, j: (i, j)),
        out_shape=jax.ShapeDtypeStruct((s, f), BF16),
        scratch_shapes=[pltpu.VMEM((tm, d), BF16), pltpu.VMEM((nj, SUBLANES, tn), F32)],
        compiler_params=_params(("arbitrary", "arbitrary")),
        name="ffn_up",
    )(x, g, w_up, w_up, conv_w, conv_b)


def kernel(x, mem, w_in, conv_mix_w, hg_lower_bounds, hg_norm_w, w_mem_kv, w_branch, w_out,
           norm_mix_pre, norm_mix_post, norm_mem, norm_ffn_pre, norm_ffn_post,
           w_ffn_up, conv_ffn_w, conv_ffn_b, w_ffn_down):
    bsz, s, d = x.shape
    depth = w_in.shape[0]
    conv_width = conv_mix_w.shape[2]
    hg_width = hg_norm_w.shape[1]
    mem_width = w_mem_kv.shape[2] // 2
    assert conv_width == hg_width == mem_width == IN_TILE
    ya_col = 0
    hg_col0 = IN_TILE
    ym_col = hg_col0 + 4 * hg_width
    gate_col0 = ym_col + mem_width
    lower_bounds = hg_lower_bounds.astype(F32)

    def rows(p):
        return p.reshape(depth, 1, p.shape[1])

    w_br_b, w_dn_b = w_branch.astype(BF16), w_ffn_down.astype(BF16)
    g_mix_pre, g_mix_post, g_mem, g_ffn_pre, g_ffn_post, hg_nw, ffn_b = (
        rows(p) for p in (norm_mix_pre, norm_mix_post, norm_mem, norm_ffn_pre, norm_ffn_post,
                          hg_norm_w, conv_ffn_b))

    outs = []
    for b in range(bsz):
        xb = x[b]
        mb = mem[b]
        for l in range(depth):
            mem_kv = _norm_proj(mb, g_mem, w_mem_kv, l, MEM_ROWS, IN_TILE)
            proj = _in_proj(xb, g_mix_pre, w_in, conv_mix_w, mem_kv, l, ROW_TILE)
            y_b = _hgrn(proj, lower_bounds, hg_nw, l, HGRN_ROWS, hg_col0)
            merged = _merge(proj, y_b, w_br_b, l, ROW_TILE, IN_TILE, ya_col, ym_col, gate_col0)
            xb = _proj_norm_res(merged, w_out, xb, g_mix_post, l, OUT_PROJ_ROWS, d)
            act = _ffn_up(xb, g_ffn_pre, w_ffn_up, conv_ffn_w, ffn_b, l, ROW_TILE, FFN_TILE)
            xb = _proj_norm_res(act, w_dn_b, xb, g_ffn_post, l, ROW_TILE, FFN_DOWN_K)
        outs.append(xb)
    return outs[0][None] if bsz == 1 else jnp.stack(outs)
```

```python
import functools

import numpy as np
import jax
import jax.numpy as jnp
from jax import lax
from jax.experimental import pallas as pl
from jax.experimental.pallas import tpu as pltpu

F32 = jnp.float32
BF16 = jnp.bfloat16

RMS_EPS = 1e-6
LOG_TINY = 1e-30
CONV_K = 3
HG_HEADS = 8
HG_DIM = 128
HG_CHUNK = 64
HG_LEVELS = 6
HG_GROUP = 16
LOG2E = 1.4426950408889634
MEM_HEADS = 4
SUBLANES = 8
VMEM_LIMIT = 56 * 1024 * 1024

ROW_TILE = 1024
OUT_PROJ_ROWS = 512
FFN_TILE = 512
FFN_DOWN_K = 1408
MEM_ROWS = 256
ROW_SLAB = 256
HGRN_ROWS = 2 * HG_GROUP * HG_CHUNK


def _params(semantics):
    return pltpu.CompilerParams(dimension_semantics=semantics, vmem_limit_bytes=VMEM_LIMIT)


def _dot(a, b):
    return jnp.dot(a, b, preferred_element_type=F32)


def _dot_nt(a, b):
    return lax.dot_general(a, b, (((1,), (1,)), ((), ())), preferred_element_type=F32)


def _dot_tn(a, b):
    return lax.dot_general(a, b, (((0,), (0,)), ((), ())), preferred_element_type=F32)


def _rms_scale(x):
    return lax.rsqrt(jnp.mean(x * x, axis=-1, keepdims=True) + RMS_EPS)


def _sigmoid(x):
    return 1.0 / (1.0 + jnp.exp2(x * -LOG2E))


def _store_normed(x_ref, g_ref, h_ref, rows):
    tm = x_ref.shape[0]

    def body(r, _):
        sl = pl.ds(pl.multiple_of(r * rows, rows), rows)
        x = x_ref[sl, :]
        h_ref[sl, :] = (x * _rms_scale(x) * g_ref[...]).astype(BF16)
        return 0

    lax.fori_loop(0, tm // rows, body, 0)


def _norm_proj_kernel(x_ref, g_ref, w_ref, o_ref, h_ref):
    @pl.when(pl.program_id(1) == 0)
    def _():
        _store_normed(x_ref, g_ref, h_ref, min(ROW_SLAB, x_ref.shape[0]))

    o_ref[...] = _dot(h_ref[...], w_ref[...].astype(BF16)).astype(o_ref.dtype)


def _norm_proj(x, g, w, layer, tm, tn):
    s, d = x.shape
    n = w.shape[2]
    tm, tn = min(tm, s), min(tn, n)
    return pl.pallas_call(
        _norm_proj_kernel,
        grid=(s // tm, n // tn),
        in_specs=[pl.BlockSpec((tm, d), lambda i, j: (i, 0)),
                  pl.BlockSpec((None, 1, d), lambda i, j: (layer, 0, 0)),
                  pl.BlockSpec((None, d, tn), lambda i, j: (layer, 0, j))],
        out_specs=pl.BlockSpec((tm, tn), lambda i, j: (i, j)),
        out_shape=jax.ShapeDtypeStruct((s, n), BF16),
        scratch_shapes=[pltpu.VMEM((tm, d), BF16)],
        compiler_params=_params(("arbitrary", "arbitrary")),
        name="norm_proj",
    )(x, g, w)


def _conv3_rows(u, halo, w):
    tm = u.shape[0]
    u1 = pltpu.roll(u, 1, 0)
    u2 = pltpu.roll(u, 2, 0)
    y = u2 * w[0:1, :] + u1 * w[1:2, :] + u * w[2:3, :]
    head = jnp.concatenate([halo, u[0:SUBLANES, :]], axis=0)
    h1 = pltpu.roll(head, 1, 0)[SUBLANES:, :]
    h2 = pltpu.roll(head, 2, 0)[SUBLANES:, :]
    y_head = h2 * w[0:1, :] + h1 * w[1:2, :] + u[0:SUBLANES, :] * w[2:3, :]
    if tm == SUBLANES:
        return y_head
    return jnp.concatenate([y_head, y[SUBLANES:, :]], axis=0)


IN_TILE = 1024
IN_SKIP = 2


def _in_proj_kernel(x_ref, g_ref, w_ref, cw_ref, kv_ref, o_ref, h_ref, stash_ref, halo_ref):
    i = pl.program_id(0)
    j = pl.program_id(1)
    tm, tn = o_ref.shape
    hd = tn // MEM_HEADS

    @pl.when(j == 0)
    def _():
        _store_normed(x_ref, g_ref, h_ref, min(ROW_SLAB, tm))

    @pl.when(jnp.logical_and(i == 0, j == 0))
    def _():
        halo_ref[...] = jnp.zeros_like(halo_ref)

    def proj(lo, width):
        return _dot(h_ref[...], w_ref[:, lo:lo + width].astype(BF16))

    @pl.when(j < 2)
    def _():
        stash_ref[j] = proj(0, tn).astype(BF16)

    @pl.when(j == 2)
    def _():
        for lo in range(0, tn, hd):
            cols = slice(lo, lo + hd)
            u = stash_ref[1, :, cols].astype(F32) * proj(lo, hd)
            y = _conv3_rows(u, halo_ref[:, cols], cw_ref[:, cols])
            halo_ref[:, cols] = u[tm - SUBLANES:, :]
            o_ref[:, cols] = (stash_ref[0, :, cols].astype(F32) * y).astype(o_ref.dtype)

    @pl.when(jnp.logical_and(j > 2, j != 7))
    def _():
        o_ref[...] = proj(0, tn).astype(o_ref.dtype)

    @pl.when(j == 7)
    def _():
        wb = w_ref[...].astype(BF16)
        slab = min(ROW_SLAB, tm)
        heads = range(MEM_HEADS)
        probs = None
        for r in range(tm // slab + 1):
            if r < tm // slab:
                q = _dot(h_ref[r * slab:(r + 1) * slab, :], wb).astype(BF16)
                scores = [_dot_nt(q[:, h * hd:(h + 1) * hd], kv_ref[:, h * hd:(h + 1) * hd])
                          * (hd ** -0.5) for h in heads]
            if probs is not None:
                for h in heads:
                    v = kv_ref[:, tn + h * hd:tn + (h + 1) * hd]
                    o_ref[(r - 1) * slab:r * slab, h * hd:(h + 1) * hd] = (
                        _dot(probs[h], v).astype(o_ref.dtype))
            if r < tm // slab:
                probs = []
                for sc in scores:
                    p = jnp.exp(sc - jnp.max(sc, axis=-1, keepdims=True))
                    probs.append((p / jnp.sum(p, axis=-1, keepdims=True)).astype(BF16))


def _in_proj(x, g, w, conv_w, mem_kv, layer, tm):
    s, d = x.shape
    n = w.shape[2]
    tm, tn = min(tm, s), IN_TILE
    assert conv_w.shape[2] == tn and mem_kv.shape[1] == 2 * tn
    return pl.pallas_call(
        _in_proj_kernel,
        grid=(s // tm, n // tn),
        in_specs=[pl.BlockSpec((tm, d), lambda i, j: (i, 0)),
                  pl.BlockSpec((None, 1, d), lambda i, j: (layer, 0, 0)),
                  pl.BlockSpec((None, d, tn), lambda i, j: (layer, 0, j)),
                  pl.BlockSpec((None, CONV_K, tn), lambda i, j: (layer, 0, 0)),
                  pl.BlockSpec(mem_kv.shape, lambda i, j: (0, 0))],
        out_specs=pl.BlockSpec((tm, tn), lambda i, j: (i, jnp.maximum(j - IN_SKIP, 0))),
        out_shape=jax.ShapeDtypeStruct((s, n - IN_SKIP * tn), BF16),
        scratch_shapes=[pltpu.VMEM((tm, d), BF16), pltpu.VMEM((2, tm, tn), BF16),
                        pltpu.VMEM((SUBLANES, tn), F32)],
        compiler_params=_params(("arbitrary", "arbitrary")),
        name="in_proj",
    )(x, g, w, conv_w, mem_kv)


def _hgrn_pair_levels():
    i = np.arange(HG_CHUNK)[:, None]
    j = np.arange(HG_CHUNK)[None, :]
    lvl = np.full((HG_CHUNK, HG_CHUNK), HG_LEVELS + 1, np.int32)
    lvl[i == j] = 0
    for level in range(1, HG_LEVELS + 1):
        lvl[((i >> level) == (j >> level)) & (((i >> (level - 1)) & 1) == 1)
            & (((j >> (level - 1)) & 1) == 0)] = level
    return lvl


def _level_operand(q, k, b, lvl):
    size = 1 << lvl
    half = size >> 1
    if half >= SUBLANES:
        parts = []
        for lo in range(0, HG_CHUNK, size):
            mid, hi = lo + half, lo + size
            bm = b[mid - 1:mid, :]
            parts.append(k[lo:mid, :] * jnp.exp(bm - b[lo:mid, :]))
            parts.append(q[mid:hi, :] * jnp.exp(b[mid:hi, :] - bm))
        return jnp.concatenate(parts, axis=0).astype(BF16)
    groups = HG_CHUNK // SUBLANES
    shape3 = (groups, SUBLANES, HG_DIM)
    b3, q3, k3 = b.reshape(shape3), q.reshape(shape3), k.reshape(shape3)
    sub = lax.broadcasted_iota(jnp.int32, (1, SUBLANES, HG_DIM), 1)
    upper = ((sub >> (lvl - 1)) & 1) == 1
    if size == 2:
        bm = pltpu.roll(b, 1, 0).reshape(shape3)
        x = jnp.where(upper, (b3 - bm) * LOG2E, 0.0)
    else:
        if size == 8:
            bm = jnp.broadcast_to(b3[:, 3:4, :], shape3)
        else:
            bm = jnp.where(sub < 4, jnp.broadcast_to(b3[:, 1:2, :], shape3),
                           jnp.broadcast_to(b3[:, 5:6, :], shape3))
        x = (b3 - bm) * jnp.where(upper, LOG2E, -LOG2E)
    a = jnp.where(upper, q3, k3) * jnp.exp2(x)
    return a.reshape(HG_CHUNK, HG_DIM).astype(BF16)


def _hgrn_kernel(layer, q_ref, f_ref, i_ref, g_ref, lb_ref, nw_ref, tri_ref, lvl_ref,
                 o_ref, st_ref):
    @pl.when(pl.program_id(1) == 0)
    def _():
        st_ref[...] = jnp.zeros_like(st_ref)

    lbs = lb_ref[...]
    e = jnp.exp(lbs - jnp.max(lbs, axis=0, keepdims=True))
    soft = e / jnp.sum(e, axis=0, keepdims=True)
    lb = jnp.zeros((1, HG_DIM), F32)
    for r in range(1, layer + 1):
        lb = lb + soft[r:r + 1, :]
    lb = jnp.clip(lb, 0.0, 1.0)
    one_minus_lb = 1.0 - lb
    lb_tiny = lb + LOG_TINY
    nw = nw_ref[...]
    tri = tri_ref[...]
    pair_level = lvl_ref[...]
    chunks = range(HG_GROUP)

    def group(gi, _):
        rows = [pl.ds(pl.multiple_of((gi * HG_GROUP + c) * HG_CHUNK, HG_CHUNK), HG_CHUNK)
                for c in chunks]

        q, k, b = [], [], []
        for c in chunks:
            zq = q_ref[rows[c], :].astype(F32)
            zf = f_ref[rows[c], :].astype(F32)
            q.append(zq * _sigmoid(zq) * (HG_DIM ** -0.5))
            sig = _sigmoid(zf)
            logf = jnp.log(lb_tiny + one_minus_lb * sig)
            k.append(one_minus_lb * (1.0 - sig) - LOG_TINY)
            p0 = logf.astype(BF16)
            r0 = logf - p0.astype(F32)
            p1 = r0.astype(BF16)
            p2 = (r0 - p1.astype(F32)).astype(BF16)
            cum3 = _dot(tri, jnp.concatenate([p0, p1, p2], axis=1))
            b.append(cum3[:, :HG_DIM] + cum3[:, HG_DIM:2 * HG_DIM] + cum3[:, 2 * HG_DIM:])

        scores = []
        for c in chunks:
            s = jnp.where(pair_level == 0, _dot_nt(q[c].astype(BF16), k[c].astype(BF16)), 0.0)
            for lvl in range(1, HG_LEVELS + 1):
                a = _level_operand(q[c], k[c], b[c], lvl)
                half = 1 << (lvl - 1)
                if half < SUBLANES:
                    s = jnp.where(pair_level == lvl, _dot_nt(a, a), s)
                    continue
                ups = [(lo + half, lo + 2 * half) for lo in range(0, HG_CHUNK, 2 * half)]
                s_up = _dot_nt(jnp.concatenate([a[m:h, :] for m, h in ups], axis=0), a)
                parts, at, row = [], 0, 0
                for m, h in ups:
                    parts.append(s[row:m, :])
                    parts.append(jnp.where(pair_level[m:h, :] == lvl, s_up[at:at + half, :], s[m:h, :]))
                    at, row = at + half, h
                s = jnp.concatenate(parts, axis=0)
            scores.append(s.astype(BF16))

        v = [i_ref[rows[c], :] for c in chunks]
        b_last = [b[c][HG_CHUNK - 1:HG_CHUNK, :] for c in chunks]
        kv = [_dot_tn(v[c], (k[c] * jnp.exp(b_last[c] - b[c])).astype(BF16)) for c in chunks]
        st = st_ref[...]
        o_inter = []
        for c in chunks:
            o_inter.append(_dot_nt((q[c] * jnp.exp(b[c])).astype(BF16), st.astype(BF16)))
            st = st * jnp.exp(b_last[c]) + kv[c]
        st_ref[...] = st

        for c in chunks:
            o = _dot(scores[c], v[c]) + o_inter[c]
            zg = g_ref[rows[c], :].astype(F32)
            o = o * _rms_scale(o) * nw
            o_ref[rows[c], :] = (o * (zg * _sigmoid(zg))).astype(o_ref.dtype)
        return 0

    lax.fori_loop(0, q_ref.shape[0] // (HG_CHUNK * HG_GROUP), group, 0)


def _hgrn(proj, lower_bounds, norm_w, layer, tm, col0):
    s = proj.shape[0]
    tm = min(tm, s)
    assert tm % (HG_CHUNK * HG_GROUP) == 0
    depth = lower_bounds.shape[0]
    width = HG_HEADS * HG_DIM
    cb = col0 // HG_DIM
    tri = jnp.asarray(np.tril(np.ones((HG_CHUNK, HG_CHUNK), np.float32)), BF16)
    pair_levels = jnp.asarray(_hgrn_pair_levels())

    def col(k):
        return pl.BlockSpec((tm, HG_DIM), lambda h, i: (i, cb + k * HG_HEADS + h))

    return pl.pallas_call(
        functools.partial(_hgrn_kernel, layer),
        grid=(HG_HEADS, s // tm),
        in_specs=[col(0), col(1), col(2), col(3),
                  pl.BlockSpec((depth, HG_DIM), lambda h, i: (0, h)),
                  pl.BlockSpec((None, 1, HG_DIM), lambda h, i: (layer, 0, h)),
                  pl.BlockSpec((HG_CHUNK, HG_CHUNK), lambda h, i: (0, 0)),
                  pl.BlockSpec((HG_CHUNK, HG_CHUNK), lambda h, i: (0, 0))],
        out_specs=pl.BlockSpec((tm, HG_DIM), lambda h, i: (i, h)),
        out_shape=jax.ShapeDtypeStruct((s, width), BF16),
        scratch_shapes=[pltpu.VMEM((HG_DIM, HG_DIM), F32)],
        compiler_params=_params(("arbitrary", "arbitrary")),
        name="hgrn2",
    )(proj, proj, proj, proj, lower_bounds, norm_w, tri, pair_levels)


def _merge_kernel(ya_ref, yb_ref, ym_ref, ga_ref, gb_ref, gm_ref, w_ref, o_ref):
    acc = _sigmoid(ga_ref[...].astype(F32)) * _dot(ya_ref[...], w_ref[0])
    acc = acc + _sigmoid(gb_ref[...].astype(F32)) * _dot(yb_ref[...], w_ref[1])
    acc = acc + _sigmoid(gm_ref[...].astype(F32)) * _dot(ym_ref[...], w_ref[2])
    o_ref[...] = acc.astype(o_ref.dtype)


def _merge(proj, y_b, w_branch, layer, tm, tn, ya_col, ym_col, col0):
    s, c = y_b.shape
    d = w_branch.shape[3]
    tm, tn = min(tm, s), min(tn, d)
    gcb = col0 // tn
    nj = d // tn

    def gate(k):
        return pl.BlockSpec((tm, tn), lambda j, i: (i, gcb + k * nj + j))

    def branch(col):
        return pl.BlockSpec((tm, c), lambda j, i: (i, col // c))

    return pl.pallas_call(
        _merge_kernel,
        grid=(nj, s // tm),
        in_specs=[branch(ya_col), branch(0), branch(ym_col), gate(0), gate(1), gate(2),
                  pl.BlockSpec((None, 3, c, tn), lambda j, i: (layer, 0, 0, j))],
        out_specs=pl.BlockSpec((tm, tn), lambda j, i: (i, j)),
        out_shape=jax.ShapeDtypeStruct((s, d), BF16),
        compiler_params=_params(("arbitrary", "arbitrary")),
        name="merge",
    )(proj, y_b, proj, proj, proj, proj, w_branch)


def _proj_norm_res_single_kernel(a_ref, w_ref, x_ref, g_ref, o_ref):
    y = _dot(a_ref[...], w_ref[...].astype(BF16))
    o_ref[...] = x_ref[...] + y * _rms_scale(y) * g_ref[...]


def _proj_norm_res_kernel(a_ref, w_ref, x_ref, g_ref, o_ref):
    kk = pl.program_id(1)

    @pl.when(kk == 0)
    def _():
        o_ref[...] = _dot(a_ref[...], w_ref[...])

    last = pl.num_programs(1) - 1

    @pl.when(jnp.logical_and(kk > 0, kk < last))
    def _():
        o_ref[...] += _dot(a_ref[...], w_ref[...])

    @pl.when(kk == last)
    def _():
        slab = min(ROW_SLAB, o_ref.shape[0])
        for r in range(o_ref.shape[0] // slab):
            rows = slice(r * slab, (r + 1) * slab)
            acc = o_ref[rows, :] + _dot(a_ref[rows, :], w_ref[...])
            o_ref[rows, :] = x_ref[rows, :] + acc * _rms_scale(acc) * g_ref[...]


def _proj_norm_res(a, w, x, g, layer, tm, tk):
    s, k = a.shape
    d = w.shape[2]
    tm, tk = min(tm, s), min(tk, k)
    return pl.pallas_call(
        _proj_norm_res_single_kernel if tk == k else _proj_norm_res_kernel,
        grid=(s // tm, k // tk),
        in_specs=[pl.BlockSpec((tm, tk), lambda i, kk: (i, kk)),
                  pl.BlockSpec((None, tk, d), lambda i, kk: (layer, kk, 0)),
                  pl.BlockSpec((tm, d), lambda i, kk: (i, 0)),
                  pl.BlockSpec((None, 1, d), lambda i, kk: (layer, 0, 0))],
        out_specs=pl.BlockSpec((tm, d), lambda i, kk: (i, 0)),
        out_shape=jax.ShapeDtypeStruct((s, d), F32),
        compiler_params=_params(("arbitrary", "arbitrary")),
        name="proj_norm_res",
    )(a, w, x, g)


def _gelu_tanh(x):
    return 0.5 * x * (1.0 + jnp.tanh(np.sqrt(2.0 / np.pi) * (x + 0.044715 * (x * x * x))))


def _ffn_up_kernel(x_ref, g_ref, wg_ref, wv_ref, cw_ref, cb_ref, o_ref, h_ref, halo_ref):
    i = pl.program_id(0)
    j = pl.program_id(1)

    @pl.when(j == 0)
    def _():
        _store_normed(x_ref, g_ref, h_ref, min(ROW_SLAB, x_ref.shape[0]))

    @pl.when(i == 0)
    def _():
        halo_ref[j] = jnp.zeros(halo_ref.shape[1:], F32)

    tm = x_ref.shape[0]
    h = h_ref[...]
    up_g = _dot(h, wg_ref[...].astype(BF16))
    z = _conv3_rows(up_g, halo_ref[j], cw_ref[...]) + cb_ref[...]
    halo_ref[j] = up_g[tm - SUBLANES:, :]
    o_ref[...] = (_gelu_tanh(z) * _dot(h, wv_ref[...].astype(BF16))).astype(o_ref.dtype)


def _ffn_up(x, g, w_up, conv_w, conv_b, layer, tm, tn):
    s, d = x.shape
    f = conv_w.shape[2]
    tm, tn = min(tm, s), min(tn, f)
    nj = f // tn
    return pl.pallas_call(
        _ffn_up_kernel,
        grid=(s // tm, nj),
        in_specs=[pl.BlockSpec((tm, d), lambda i, j: (i, 0)),
                  pl.BlockSpec((None, 1, d), lambda i, j: (layer, 0, 0)),
                  pl.BlockSpec((None, d, tn), lambda i, j: (layer, 0, j)),
                  pl.BlockSpec((None, d, tn), lambda i, j: (layer, 0, nj + j)),
                  pl.BlockSpec((None, CONV_K, tn), lambda i, j: (layer, 0, j)),
                  pl.BlockSpec((None, 1, tn), lambda i, j: (layer, 0, j))],
        out_specs=pl.BlockSpec((tm, tn), lambda i, j: (i, j)),
        out_shape=jax.ShapeDtypeStruct((s, f), BF16),
        scratch_shapes=[pltpu.VMEM((tm, d), BF16), pltpu.VMEM((nj, SUBLANES, tn), F32)],
        compiler_params=_params(("arbitrary", "arbitrary")),
        name="ffn_up",
    )(x, g, w_up, w_up, conv_w, conv_b)


def kernel(x, mem, w_in, conv_mix_w, hg_lower_bounds, hg_norm_w, w_mem_kv, w_branch, w_out,
           norm_mix_pre, norm_mix_post, norm_mem, norm_ffn_pre, norm_ffn_post,
           w_ffn_up, conv_ffn_w, conv_ffn_b, w_ffn_down):
    bsz, s, d = x.shape
    depth = w_in.shape[0]
    conv_width = conv_mix_w.shape[2]
    hg_width = hg_norm_w.shape[1]
    mem_width = w_mem_kv.shape[2] // 2
    assert conv_width == hg_width == mem_width == IN_TILE
    ya_col = 0
    hg_col0 = IN_TILE
    ym_col = hg_col0 + 4 * hg_width
    gate_col0 = ym_col + mem_width
    lower_bounds = hg_lower_bounds.astype(F32)

    def rows(p):
        return p.reshape(depth, 1, p.shape[1])

    w_br_b, w_dn_b = w_branch.astype(BF16), w_ffn_down.astype(BF16)
    g_mix_pre, g_mix_post, g_mem, g_ffn_pre, g_ffn_post, hg_nw, ffn_b = (
        rows(p) for p in (norm_mix_pre, norm_mix_post, norm_mem, norm_ffn_pre, norm_ffn_post,
                          hg_norm_w, conv_ffn_b))

    outs = []
    for b in range(bsz):
        xb = x[b]
        mb = mem[b]
        for l in range(depth):
            mem_kv = _norm_proj(mb, g_mem, w_mem_kv, l, MEM_ROWS, IN_TILE)
            proj = _in_proj(xb, g_mix_pre, w_in, conv_mix_w, mem_kv, l, ROW_TILE)
            y_b = _hgrn(proj, lower_bounds, hg_nw, l, HGRN_ROWS, hg_col0)
            merged = _merge(proj, y_b, w_br_b, l, ROW_TILE, IN_TILE, ya_col, ym_col, gate_col0)
            xb = _proj_norm_res(merged, w_out, xb, g_mix_post, l, OUT_PROJ_ROWS, d)
            act = _ffn_up(xb, g_ffn_pre, w_ffn_up, conv_ffn_w, ffn_b, l, ROW_TILE, FFN_TILE)
            xb = _proj_norm_res(act, w_dn_b, xb, g_ffn_post, l, ROW_TILE, FFN_DOWN_K)
        outs.append(xb)
    return outs[0][None] if bsz == 1 else jnp.stack(outs)
```

```python
import functools

import numpy as np
import jax
import jax.numpy as jnp
from jax import lax
from jax.experimental import pallas as pl
from jax.experimental.pallas import tpu as pltpu

F32 = jnp.float32
BF16 = jnp.bfloat16

RMS_EPS = 1e-6
LOG_TINY = 1e-30
CONV_K = 3
HG_HEADS = 8
HG_DIM = 128
HG_CHUNK = 64
HG_LEVELS = 6
HG_GROUP = 16
LOG2E = 1.4426950408889634
MEM_HEADS = 4
SUBLANES = 8
VMEM_LIMIT = 56 * 1024 * 1024

ROW_TILE = 1024
OUT_PROJ_ROWS = 512
FFN_TILE = 512
FFN_DOWN_K = 1408
MEM_ROWS = 256
ROW_SLAB = 256
HGRN_ROWS = 2 * HG_GROUP * HG_CHUNK

def _params(semantics):
    return pltpu.CompilerParams(dimension_semantics=semantics, vmem_limit_bytes=VMEM_LIMIT)


def _dot(a, b):
    return jnp.dot(a, b, preferred_element_type=F32)


def _dot_nt(a, b):
    return lax.dot_general(a, b, (((1,), (1,)), ((), ())), preferred_element_type=F32)


def _dot_tn(a, b):
    return lax.dot_general(a, b, (((0,), (0,)), ((), ())), preferred_element_type=F32)


def _rms_scale(x):
    return lax.rsqrt(jnp.mean(x * x, axis=-1, keepdims=True) + RMS_EPS)


def _sigmoid(x):
    return 1.0 / (1.0 + jnp.exp2(x * -LOG2E))


def _store_normed(x_ref, g_ref, h_ref, rows):
    tm = x_ref.shape[0]

    def body(r, _):
        sl = pl.ds(pl.multiple_of(r * rows, rows), rows)
        x = x_ref[sl, :]
        h_ref[sl, :] = (x * _rms_scale(x) * g_ref[...]).astype(BF16)
        return 0

    lax.fori_loop(0, tm // rows, body, 0)


def _norm_proj_kernel(x_ref, g_ref, w_ref, o_ref, h_ref):
    @pl.when(pl.program_id(1) == 0)
    def _():
        _store_normed(x_ref, g_ref, h_ref, min(ROW_SLAB, x_ref.shape[0]))

    o_ref[...] = _dot(h_ref[...], w_ref[...].astype(BF16)).astype(o_ref.dtype)


def _norm_proj(x, g, w, layer, tm, tn):
    s, d = x.shape
    n = w.shape[2]
    tm, tn = min(tm, s), min(tn, n)
    return pl.pallas_call(
        _norm_proj_kernel,
        grid=(s // tm, n // tn),
        in_specs=[pl.BlockSpec((tm, d), lambda i, j: (i, 0)),
                  pl.BlockSpec((None, 1, d), lambda i, j: (layer, 0, 0)),
                  pl.BlockSpec((None, d, tn), lambda i, j: (layer, 0, j))],
        out_specs=pl.BlockSpec((tm, tn), lambda i, j: (i, j)),
        out_shape=jax.ShapeDtypeStruct((s, n), BF16),
        scratch_shapes=[pltpu.VMEM((tm, d), BF16)],
        compiler_params=_params(("arbitrary", "arbitrary")),
        name="norm_proj",
    )(x, g, w)


def _conv3_rows(u, halo, w):
    tm = u.shape[0]
    u1 = pltpu.roll(u, 1, 0)
    u2 = pltpu.roll(u, 2, 0)
    y = u2 * w[0:1, :] + u1 * w[1:2, :] + u * w[2:3, :]
    head = jnp.concatenate([halo, u[0:SUBLANES, :]], axis=0)
    h1 = pltpu.roll(head, 1, 0)[SUBLANES:, :]
    h2 = pltpu.roll(head, 2, 0)[SUBLANES:, :]
    y_head = h2 * w[0:1, :] + h1 * w[1:2, :] + u[0:SUBLANES, :] * w[2:3, :]
    if tm == SUBLANES:
        return y_head
    return jnp.concatenate([y_head, y[SUBLANES:, :]], axis=0)


IN_TILE = 1024
IN_SKIP = 2


def _in_proj_kernel(x_ref, g_ref, w_ref, cw_ref, kv_ref, o_ref, h_ref, stash_ref, halo_ref):
    i = pl.program_id(0)
    j = pl.program_id(1)
    tm, tn = o_ref.shape
    hd = tn // MEM_HEADS

    @pl.when(j == 0)
    def _():
        _store_normed(x_ref, g_ref, h_ref, min(ROW_SLAB, tm))

    @pl.when(jnp.logical_and(i == 0, j == 0))
    def _():
        halo_ref[...] = jnp.zeros_like(halo_ref)

    def proj(lo, width):
        return _dot(h_ref[...], w_ref[:, lo:lo + width].astype(BF16))

    @pl.when(j < 2)
    def _():
        stash_ref[j] = proj(0, tn).astype(BF16)

    @pl.when(j == 2)
    def _():
        for lo in range(0, tn, hd):
            cols = slice(lo, lo + hd)
            u = stash_ref[1, :, cols].astype(F32) * proj(lo, hd)
            y = _conv3_rows(u, halo_ref[:, cols], cw_ref[:, cols])
            halo_ref[:, cols] = u[tm - SUBLANES:, :]
            o_ref[:, cols] = (stash_ref[0, :, cols].astype(F32) * y).astype(o_ref.dtype)

    @pl.when(jnp.logical_and(j > 2, j != 7))
    def _():
        o_ref[...] = proj(0, tn).astype(o_ref.dtype)

    @pl.when(j == 7)
    def _():
        wb = w_ref[...].astype(BF16)
        slab = min(ROW_SLAB, tm)
        heads = range(MEM_HEADS)
        probs = None
        for r in range(tm // slab + 1):
            if r < tm // slab:
                q = _dot(h_ref[r * slab:(r + 1) * slab, :], wb).astype(BF16)
                scores = [_dot_nt(q[:, h * hd:(h + 1) * hd], kv_ref[:, h * hd:(h + 1) * hd])
                          * (hd ** -0.5) for h in heads]
            if probs is not None:
                for h in heads:
                    v = kv_ref[:, tn + h * hd:tn + (h + 1) * hd]
                    o_ref[(r - 1) * slab:r * slab, h * hd:(h + 1) * hd] = (
                        _dot(probs[h], v).astype(o_ref.dtype))
            if r < tm // slab:
                probs = []
                for sc in scores:
                    p = jnp.exp(sc - jnp.max(sc, axis=-1, keepdims=True))
                    probs.append((p / jnp.sum(p, axis=-1, keepdims=True)).astype(BF16))


def _in_proj(x, g, w, conv_w, mem_kv, layer, tm):
    s, d = x.shape
    n = w.shape[2]
    tm, tn = min(tm, s), IN_TILE
    assert conv_w.shape[2] == tn and mem_kv.shape[1] == 2 * tn
    return pl.pallas_call(
        _in_proj_kernel,
        grid=(s // tm, n // tn),
        in_specs=[pl.BlockSpec((tm, d), lambda i, j: (i, 0)),
                  pl.BlockSpec((None, 1, d), lambda i, j: (layer, 0, 0)),
                  pl.BlockSpec((None, d, tn), lambda i, j: (layer, 0, j)),
                  pl.BlockSpec((None, CONV_K, tn), lambda i, j: (layer, 0, 0)),
                  pl.BlockSpec(mem_kv.shape, lambda i, j: (0, 0))],
        out_specs=pl.BlockSpec((tm, tn), lambda i, j: (i, jnp.maximum(j - IN_SKIP, 0))),
        out_shape=jax.ShapeDtypeStruct((s, n - IN_SKIP * tn), BF16),
        scratch_shapes=[pltpu.VMEM((tm, d), BF16), pltpu.VMEM((2, tm, tn), BF16),
                        pltpu.VMEM((SUBLANES, tn), F32)],
        compiler_params=_params(("arbitrary", "arbitrary")),
        name="in_proj",
    )(x, g, w, conv_w, mem_kv)


def _hgrn_pair_levels():
    i = np.arange(HG_CHUNK)[:, None]
    j = np.arange(HG_CHUNK)[None, :]
    lvl = np.full((HG_CHUNK, HG_CHUNK), HG_LEVELS + 1, np.int32)
    lvl[i == j] = 0
    for level in range(1, HG_LEVELS + 1):
        lvl[((i >> level) == (j >> level)) & (((i >> (level - 1)) & 1) == 1)
            & (((j >> (level - 1)) & 1) == 0)] = level
    return lvl


def _level_operand(q, k, b, lvl):
    size = 1 << lvl
    half = size >> 1
    if half >= SUBLANES:
        parts = []
        for lo in range(0, HG_CHUNK, size):
            mid, hi = lo + half, lo + size
            bm = b[mid - 1:mid, :]
            parts.append(k[lo:mid, :] * jnp.exp(bm - b[lo:mid, :]))
            parts.append(q[mid:hi, :] * jnp.exp(b[mid:hi, :] - bm))
        return jnp.concatenate(parts, axis=0).astype(BF16)
    groups = HG_CHUNK // SUBLANES
    shape3 = (groups, SUBLANES, HG_DIM)
    b3, q3, k3 = b.reshape(shape3), q.reshape(shape3), k.reshape(shape3)
    sub = lax.broadcasted_iota(jnp.int32, (1, SUBLANES, HG_DIM), 1)
    upper = ((sub >> (lvl - 1)) & 1) == 1
    if size == 2:
        bm = pltpu.roll(b, 1, 0).reshape(shape3)
        x = jnp.where(upper, (b3 - bm) * LOG2E, 0.0)
    else:
        if size == 8:
            bm = jnp.broadcast_to(b3[:, 3:4, :], shape3)
        else:
            bm = jnp.where(sub < 4, jnp.broadcast_to(b3[:, 1:2, :], shape3),
                           jnp.broadcast_to(b3[:, 5:6, :], shape3))
        x = (b3 - bm) * jnp.where(upper, LOG2E, -LOG2E)
    a = jnp.where(upper, q3, k3) * jnp.exp2(x)
    return a.reshape(HG_CHUNK, HG_DIM).astype(BF16)


def _hgrn_kernel(layer, q_ref, f_ref, i_ref, g_ref, lb_ref, nw_ref, tri_ref, lvl_ref,
                 o_ref, st_ref):
    @pl.when(pl.program_id(1) == 0)
    def _():
        st_ref[...] = jnp.zeros_like(st_ref)

    lbs = lb_ref[...]
    e = jnp.exp(lbs - jnp.max(lbs, axis=0, keepdims=True))
    soft = e / jnp.sum(e, axis=0, keepdims=True)
    lb = jnp.zeros((1, HG_DIM), F32)
    for r in range(1, layer + 1):
        lb = lb + soft[r:r + 1, :]
    lb = jnp.clip(lb, 0.0, 1.0)
    one_minus_lb = 1.0 - lb
    lb_tiny = lb + LOG_TINY
    nw = nw_ref[...]
    tri = tri_ref[...]
    pair_level = lvl_ref[...]
    chunks = range(HG_GROUP)

    def group(gi, _):
        rows = [pl.ds(pl.multiple_of((gi * HG_GROUP + c) * HG_CHUNK, HG_CHUNK), HG_CHUNK)
                for c in chunks]

        q, k, b = [], [], []
        for c in chunks:
            zq = q_ref[rows[c], :].astype(F32)
            zf = f_ref[rows[c], :].astype(F32)
            q.append(zq * _sigmoid(zq) * (HG_DIM ** -0.5))
            sig = _sigmoid(zf)
            logf = jnp.log(lb_tiny + one_minus_lb * sig)
            k.append(one_minus_lb * (1.0 - sig) - LOG_TINY)
            p0 = logf.astype(BF16)
            r0 = logf - p0.astype(F32)
            p1 = r0.astype(BF16)
            p2 = (r0 - p1.astype(F32)).astype(BF16)
            cum3 = _dot(tri, jnp.concatenate([p0, p1, p2], axis=1))
            b.append(cum3[:, :HG_DIM] + cum3[:, HG_DIM:2 * HG_DIM] + cum3[:, 2 * HG_DIM:])

        scores = []
        for c in chunks:
            s = jnp.where(pair_level == 0, _dot_nt(q[c].astype(BF16), k[c].astype(BF16)), 0.0)
            for lvl in range(1, HG_LEVELS + 1):
                a = _level_operand(q[c], k[c], b[c], lvl)
                half = 1 << (lvl - 1)
                if half < SUBLANES:
                    s = jnp.where(pair_level == lvl, _dot_nt(a, a), s)
                    continue
                ups = [(lo + half, lo + 2 * half) for lo in range(0, HG_CHUNK, 2 * half)]
                s_up = _dot_nt(jnp.concatenate([a[m:h, :] for m, h in ups], axis=0), a)
                parts, at, row = [], 0, 0
                for m, h in ups:
                    parts.append(s[row:m, :])
                    parts.append(jnp.where(pair_level[m:h, :] == lvl, s_up[at:at + half, :], s[m:h, :]))
                    at, row = at + half, h
                s = jnp.concatenate(parts, axis=0)
            scores.append(s.astype(BF16))

        v = [i_ref[rows[c], :] for c in chunks]
        b_last = [b[c][HG_CHUNK - 1:HG_CHUNK, :] for c in chunks]
        kv = [_dot_tn(v[c], (k[c] * jnp.exp(b_last[c] - b[c])).astype(BF16)) for c in chunks]
        st = st_ref[...]
        o_inter = []
        for c in chunks:
            o_inter.append(_dot_nt((q[c] * jnp.exp(b[c])).astype(BF16), st.astype(BF16)))
            st = st * jnp.exp(b_last[c]) + kv[c]
        st_ref[...] = st

        for c in chunks:
            o = _dot(scores[c], v[c]) + o_inter[c]
            zg = g_ref[rows[c], :].astype(F32)
            o = o * _rms_scale(o) * nw
            o_ref[rows[c], :] = (o * (zg * _sigmoid(zg))).astype(o_ref.dtype)
        return 0

    lax.fori_loop(0, q_ref.shape[0] // (HG_CHUNK * HG_GROUP), group, 0)


def _hgrn(proj, lower_bounds, norm_w, layer, tm, col0):
    s = proj.shape[0]
    tm = min(tm, s)
    assert tm % (HG_CHUNK * HG_GROUP) == 0
    depth = lower_bounds.shape[0]
    width = HG_HEADS * HG_DIM
    cb = col0 // HG_DIM
    tri = jnp.asarray(np.tril(np.ones((HG_CHUNK, HG_CHUNK), np.float32)), BF16)
    pair_levels = jnp.asarray(_hgrn_pair_levels())

    def col(k):
        return pl.BlockSpec((tm, HG_DIM), lambda h, i: (i, cb + k * HG_HEADS + h))

    return pl.pallas_call(
        functools.partial(_hgrn_kernel, layer),
        grid=(HG_HEADS, s // tm),
        in_specs=[col(0), col(1), col(2), col(3),
                  pl.BlockSpec((depth, HG_DIM), lambda h, i: (0, h)),
                  pl.BlockSpec((None, 1, HG_DIM), lambda h, i: (layer, 0, h)),
                  pl.BlockSpec((HG_CHUNK, HG_CHUNK), lambda h, i: (0, 0)),
                  pl.BlockSpec((HG_CHUNK, HG_CHUNK), lambda h, i: (0, 0))],
        out_specs=pl.BlockSpec((tm, HG_DIM), lambda h, i: (i, h)),
        out_shape=jax.ShapeDtypeStruct((s, width), BF16),
        scratch_shapes=[pltpu.VMEM((HG_DIM, HG_DIM), F32)],
        compiler_params=_params(("arbitrary", "arbitrary")),
        name="hgrn2",
    )(proj, proj, proj, proj, lower_bounds, norm_w, tri, pair_levels)


def _merge_kernel(ya_ref, yb_ref, ym_ref, ga_ref, gb_ref, gm_ref, w_ref, o_ref):
    acc = _sigmoid(ga_ref[...].astype(F32)) * _dot(ya_ref[...], w_ref[0].astype(BF16))
    acc = acc + _sigmoid(gb_ref[...].astype(F32)) * _dot(yb_ref[...], w_ref[1].astype(BF16))
    acc = acc + _sigmoid(gm_ref[...].astype(F32)) * _dot(ym_ref[...], w_ref[2].astype(BF16))
    o_ref[...] = acc.astype(o_ref.dtype)


def _merge(proj, y_b, w_branch, layer, tm, tn, ya_col, ym_col, col0):
    s, c = y_b.shape
    d = w_branch.shape[3]
    tm, tn = min(tm, s), min(tn, d)
    gcb = col0 // tn
    nj = d // tn

    def gate(k):
        return pl.BlockSpec((tm, tn), lambda j, i: (i, gcb + k * nj + j))

    def branch(col):
        return pl.BlockSpec((tm, c), lambda j, i: (i, col // c))

    return pl.pallas_call(
        _merge_kernel,
        grid=(nj, s // tm),
        in_specs=[branch(ya_col), branch(0), branch(ym_col), gate(0), gate(1), gate(2),
                  pl.BlockSpec((None, 3, c, tn), lambda j, i: (layer, 0, 0, j),
                               pipeline_mode=pl.Buffered(1))],
        out_specs=pl.BlockSpec((tm, tn), lambda j, i: (i, j)),
        out_shape=jax.ShapeDtypeStruct((s, d), BF16),
        compiler_params=_params(("arbitrary", "arbitrary")),
        name="merge",
    )(proj, y_b, proj, proj, proj, proj, w_branch)


def _proj_norm_res_single_kernel(a_ref, w_ref, x_ref, g_ref, o_ref):
    y = _dot(a_ref[...], w_ref[...].astype(BF16))
    o_ref[...] = x_ref[...] + y * _rms_scale(y) * g_ref[...]


def _proj_norm_res_kernel(a_ref, w_ref, x_ref, g_ref, o_ref):
    kk = pl.program_id(1)

    @pl.when(kk == 0)
    def _():
        o_ref[...] = _dot(a_ref[...], w_ref[...])

    last = pl.num_programs(1) - 1

    @pl.when(jnp.logical_and(kk > 0, kk < last))
    def _():
        o_ref[...] += _dot(a_ref[...], w_ref[...])

    @pl.when(kk == last)
    def _():
        slab = min(ROW_SLAB, o_ref.shape[0])
        for r in range(o_ref.shape[0] // slab):
            rows = slice(r * slab, (r + 1) * slab)
            acc = o_ref[rows, :] + _dot(a_ref[rows, :], w_ref[...])
            o_ref[rows, :] = x_ref[rows, :] + acc * _rms_scale(acc) * g_ref[...]


def _proj_norm_res(a, w, x, g, layer, tm, tk):
    s, k = a.shape
    d = w.shape[2]
    tm, tk = min(tm, s), min(tk, k)
    return pl.pallas_call(
        _proj_norm_res_single_kernel if tk == k else _proj_norm_res_kernel,
        grid=(s // tm, k // tk),
        in_specs=[pl.BlockSpec((tm, tk), lambda i, kk: (i, kk)),
                  pl.BlockSpec((None, tk, d), lambda i, kk: (layer, kk, 0)),
                  pl.BlockSpec((tm, d), lambda i, kk: (i, 0)),
                  pl.BlockSpec((None, 1, d), lambda i, kk: (layer, 0, 0))],
        out_specs=pl.BlockSpec((tm, d), lambda i, kk: (i, 0)),
        out_shape=jax.ShapeDtypeStruct((s, d), F32),
        compiler_params=_params(("arbitrary", "arbitrary")),
        name="proj_norm_res",
    )(a, w, x, g)


def _gelu_tanh(x):
    return 0.5 * x * (1.0 + jnp.tanh(np.sqrt(2.0 / np.pi) * (x + 0.044715 * (x * x * x))))


def _ffn_up_kernel(x_ref, g_ref, wg_ref, wv_ref, cw_ref, cb_ref, o_ref, h_ref, halo_ref):
    i = pl.program_id(0)
    j = pl.program_id(1)

    @pl.when(j == 0)
    def _():
        _store_normed(x_ref, g_ref, h_ref, min(ROW_SLAB, x_ref.shape[0]))

    @pl.when(i == 0)
    def _():
        halo_ref[j] = jnp.zeros(halo_ref.shape[1:], F32)

    tm = x_ref.shape[0]
    h = h_ref[...]
    up_g = _dot(h, wg_ref[...].astype(BF16))
    z = _conv3_rows(up_g, halo_ref[j], cw_ref[...]) + cb_ref[...]
    halo_ref[j] = up_g[tm - SUBLANES:, :]
    o_ref[...] = (_gelu_tanh(z) * _dot(h, wv_ref[...].astype(BF16))).astype(o_ref.dtype)


def _ffn_up(x, g, w_up, conv_w, conv_b, layer, tm, tn):
    s, d = x.shape
    f = conv_w.shape[2]
    tm, tn = min(tm, s), min(tn, f)
    nj = f // tn
    return pl.pallas_call(
        _ffn_up_kernel,
        grid=(s // tm, nj),
        in_specs=[pl.BlockSpec((tm, d), lambda i, j: (i, 0)),
                  pl.BlockSpec((None, 1, d), lambda i, j: (layer, 0, 0)),
                  pl.BlockSpec((None, d, tn), lambda i, j: (layer, 0, j)),
                  pl.BlockSpec((None, d, tn), lambda i, j: (layer, 0, nj + j)),
                  pl.BlockSpec((None, CONV_K, tn), lambda i, j: (layer, 0, j)),
                  pl.BlockSpec((None, 1, tn), lambda i, j: (layer, 0, j))],
        out_specs=pl.BlockSpec((tm, tn), lambda i, j: (i, j)),
        out_shape=jax.ShapeDtypeStruct((s, f), BF16),
        scratch_shapes=[pltpu.VMEM((tm, d), BF16), pltpu.VMEM((nj, SUBLANES, tn), F32)],
        compiler_params=_params(("arbitrary", "arbitrary")),
        name="ffn_up",
    )(x, g, w_up, w_up, conv_w, conv_b)


def kernel(x, mem, w_in, conv_mix_w, hg_lower_bounds, hg_norm_w, w_mem_kv, w_branch, w_out,
           norm_mix_pre, norm_mix_post, norm_mem, norm_ffn_pre, norm_ffn_post,
           w_ffn_up, conv_ffn_w, conv_ffn_b, w_ffn_down):
    bsz, s, d = x.shape
    depth = w_in.shape[0]
    conv_width = conv_mix_w.shape[2]
    hg_width = hg_norm_w.shape[1]
    mem_width = w_mem_kv.shape[2] // 2
    assert conv_width == hg_width == mem_width == IN_TILE
    ya_col = 0
    hg_col0 = IN_TILE
    ym_col = hg_col0 + 4 * hg_width
    gate_col0 = ym_col + mem_width
    lower_bounds = hg_lower_bounds.astype(F32)

    def rows(p):
        return p.reshape(depth, 1, p.shape[1])

    w_dn_b = w_ffn_down.astype(BF16)
    g_mix_pre, g_mix_post, g_mem, g_ffn_pre, g_ffn_post, hg_nw, ffn_b = (
        rows(p) for p in (norm_mix_pre, norm_mix_post, norm_mem, norm_ffn_pre, norm_ffn_post,
                          hg_norm_w, conv_ffn_b))

    outs = []
    for b in range(bsz):
        xb = x[b]
        mb = mem[b]
        for l in range(depth):
            mem_kv = _norm_proj(mb, g_mem, w_mem_kv, l, MEM_ROWS, IN_TILE)
            proj = _in_proj(xb, g_mix_pre, w_in, conv_mix_w, mem_kv, l, ROW_TILE)
            y_b = _hgrn(proj, lower_bounds, hg_nw, l, HGRN_ROWS, hg_col0)
            merged = _merge(proj, y_b, w_branch, l, ROW_TILE, IN_TILE, ya_col, ym_col, gate_col0)
            xb = _proj_norm_res(merged, w_out, xb, g_mix_post, l, OUT_PROJ_ROWS, d)
            act = _ffn_up(xb, g_ffn_pre, w_ffn_up, conv_ffn_w, ffn_b, l, ROW_TILE, FFN_TILE)
            xb = _proj_norm_res(act, w_dn_b, xb, g_ffn_post, l, ROW_TILE, FFN_DOWN_K)
        outs.append(xb)
    return outs[0][None] if bsz == 1 else jnp.stack(outs)
```
